```python
import math
import jax, jax.numpy as jnp
from jax import lax
import numpy as np

D_MODEL = 1024
BATCH = 4
SEQ = 8192
DEPTH = 1

N_HEADS_A = 8
HEAD_DIM_A = 64
N_IDX_HEADS = 8
IDX_DIM = 64
TOPK_MAX = 256
Q_BLOCK = 128
N_HEADS_B = 4
HEAD_DIM_B = 128
RET_CHUNK = 128
RET_THETA_BASE = 10000.0
D_A = N_HEADS_A * HEAD_DIM_A
D_B = N_HEADS_B * HEAD_DIM_B
D_MIX = D_A + D_B
D_FF = ((8 * D_MODEL // 3 + 255) // 256) * 256
N_BUCKETS = 32
MAX_DISTANCE = 128
EPS = 1e-6
SPLIT_SIZES = (D_A, D_A, D_A, N_IDX_HEADS * IDX_DIM, IDX_DIM, N_IDX_HEADS, D_B, D_B, D_B, D_B)
D_IN = D_A * 3 + N_IDX_HEADS * IDX_DIM + IDX_DIM + N_IDX_HEADS + D_B * 4

kernel_name = "hymba_dsa_retnet_hybrid_layer"


def rmsnorm(x, g):
    x32 = x.astype(jnp.float32)
    y = x32 * lax.rsqrt(jnp.mean(x32 * x32, axis=-1, keepdims=True) + EPS)
    return (y * g.astype(jnp.float32)).astype(x.dtype)


def layernorm(x, g, b):
    x32 = x.astype(jnp.float32)
    mu = jnp.mean(x32, axis=-1, keepdims=True)
    var = jnp.mean(jnp.square(x32 - mu), axis=-1, keepdims=True)
    y = (x32 - mu) * lax.rsqrt(var + EPS)
    return (y * g.astype(jnp.float32) + b.astype(jnp.float32)).astype(x.dtype)


def rel_bucket(dist):
    n = jnp.maximum(dist, 0)
    max_exact = N_BUCKETS // 2
    nf = jnp.maximum(n, max_exact).astype(jnp.float32)
    large = max_exact + (jnp.log(nf / max_exact) / math.log(MAX_DISTANCE / max_exact)
                         * (N_BUCKETS - max_exact)).astype(jnp.int32)
    large = jnp.minimum(large, N_BUCKETS - 1)
    return jnp.where(n < max_exact, n, large)


def rotate(x, pos):
    half = x.shape[-1] // 2
    theta = 1.0 / (RET_THETA_BASE ** jnp.linspace(0.0, 1.0, half, dtype=jnp.float32))
    ang = pos.astype(jnp.float32)[:, None] * theta[None, :]
    cos = jnp.cos(ang)[None, :, None, :]
    sin = jnp.sin(ang)[None, :, None, :]
    x32 = x.astype(jnp.float32)
    x1, x2 = x32[..., :half], x32[..., half:]
    out = jnp.concatenate([x1 * cos - x2 * sin, x2 * cos + x1 * sin], axis=-1)
    return out.astype(x.dtype)


def sparse_attention(q, k, v, q_idx, k_idx, w_idx, rel_bias, k_top):
    B, L = q.shape[0], q.shape[1]
    n_blocks = L // Q_BLOCK
    key_pos = jnp.arange(L, dtype=jnp.int32)
    b_ix = jnp.arange(B)[:, None, None]
    idx_scale = IDX_DIM ** -0.5
    attn_scale = HEAD_DIM_A ** -0.5

    def block(i):
        start = i * Q_BLOCK
        qi = lax.dynamic_slice_in_dim(q, start, Q_BLOCK, axis=1)
        qi_idx = lax.dynamic_slice_in_dim(q_idx, start, Q_BLOCK, axis=1)
        wi = lax.dynamic_slice_in_dim(w_idx, start, Q_BLOCK, axis=1)
        q_pos = start + jnp.arange(Q_BLOCK, dtype=jnp.int32)
        dots = jnp.einsum('bqhd,bsd->bqhs', qi_idx, k_idx) * idx_scale
        score = jnp.einsum('bqh,bqhs->bqs', wi, jax.nn.relu(dots)).astype(jnp.float32)
        causal = key_pos[None, :] <= q_pos[:, None]
        score = jnp.where(causal[None], score, -jnp.inf)
        _, sel = lax.top_k(score, k_top)
        k_sel = k[b_ix, sel]
        v_sel = v[b_ix, sel]
        logits = jnp.einsum('bqhd,bqkhd->bhqk', qi, k_sel).astype(jnp.float32) * attn_scale
        dist = q_pos[None, :, None] - sel
        bias = rel_bias[rel_bucket(dist)].astype(jnp.float32)
        logits = logits + jnp.transpose(bias, (0, 3, 1, 2))
        logits = jnp.where((dist >= 0)[:, None], logits, -jnp.inf)
        p = jax.nn.softmax(logits, axis=-1).astype(v.dtype)
        return jnp.einsum('bhqk,bqkhd->bqhd', p, v_sel)

    out = lax.map(block, jnp.arange(n_blocks))
    out = jnp.transpose(out, (1, 0, 2, 3, 4))
    return out.reshape(B, L, N_HEADS_A * HEAD_DIM_A)


def retention(q, k, v, g, gn_g):
    B, L, H, D = q.shape
    C = RET_CHUNK
    nc = L // C
    gamma = 1.0 - 2.0 ** (-5.0 - jnp.arange(H, dtype=jnp.float32))
    log_g = jnp.log(gamma)
    pos = jnp.arange(C, dtype=jnp.float32)
    diff = pos[:, None] - pos[None, :]
    inner_decay = jnp.where(diff >= 0, jnp.exp(log_g[:, None, None] * jnp.maximum(diff, 0.0)), 0.0)
    q_decay = jnp.exp(log_g[:, None] * (pos[None, :] + 1.0))
    k_decay = jnp.exp(log_g[:, None] * (C - 1.0 - pos[None, :]))
    chunk_decay = jnp.exp(log_g * C)
    dt = q.dtype
    inner_decay, q_decay, k_decay, chunk_decay = (a.astype(dt) for a in (inner_decay, q_decay, k_decay, chunk_decay))

    def to_chunks(t):
        return jnp.transpose(t.reshape(B, nc, C, H, D), (1, 0, 3, 2, 4))

    qc, kc, vc = to_chunks(q), to_chunks(k * (D ** -0.5)), to_chunks(v)

    def step(state, inp):
        qi, ki, vi = inp
        inner = jnp.einsum('bhid,bhjd->bhij', qi, ki) * inner_decay
        y = (jnp.einsum('bhij,bhje->bhie', inner, vi)
             + jnp.einsum('bhid,bhde->bhie', qi, state) * q_decay[None, :, :, None])
        state = (state * chunk_decay[None, :, None, None]
                 + jnp.einsum('bhjd,bhje->bhde', ki * k_decay[None, :, :, None], vi))
        return state, y

    state0 = jnp.zeros((B, H, D, D), dtype=dt)
    _, ys = lax.scan(step, state0, (qc, kc, vc))
    y = jnp.transpose(ys, (1, 0, 3, 2, 4)).reshape(B, L, H, D)
    y32 = y.astype(jnp.float32)
    y32 = y32 * lax.rsqrt(jnp.mean(y32 * y32, axis=-1, keepdims=True) + EPS)
    y = (y32.reshape(B, L, H * D) * gn_g.astype(jnp.float32)).astype(dt)
    return y * jax.nn.silu(g.reshape(B, L, H * D))


def setup_inputs(seed: int = 0) -> dict:
    key = jax.random.key(seed)
    ks = jax.random.split(key, 16)
    f32 = jnp.float32

    def nrm(k, shape, scale):
        return jax.random.normal(k, shape, f32) * scale

    return {
        "x": nrm(ks[0], (BATCH, SEQ, D_MODEL), 1.0),
        "norm_mix_g": 1.0 + nrm(ks[1], (DEPTH, D_MODEL), 0.02),
        "w_in": nrm(ks[2], (DEPTH, D_MODEL, D_IN), D_MODEL ** -0.5),
        "idx_k_ln_g": 1.0 + nrm(ks[3], (DEPTH, IDX_DIM), 0.02),
        "idx_k_ln_b": nrm(ks[4], (DEPTH, IDX_DIM), 0.02),
        "rel_bias": nrm(ks[5], (N_BUCKETS, N_HEADS_A), 0.5),
        "ret_gn_g": 1.0 + nrm(ks[6], (DEPTH, D_B), 0.02),
        "w_out": nrm(ks[7], (DEPTH, D_MIX, D_MODEL), D_MIX ** -0.5),
        "norm_ffn_g": 1.0 + nrm(ks[8], (DEPTH, D_MODEL), 0.02),
        "w_gate": nrm(ks[9], (DEPTH, D_MODEL, D_FF), D_MODEL ** -0.5),
        "w_up": nrm(ks[10], (DEPTH, D_MODEL, D_FF), D_MODEL ** -0.5),
        "w_down": nrm(ks[11], (DEPTH, D_FF, D_MODEL), D_FF ** -0.5),
        "norm_final_g": 1.0 + nrm(ks[12], (D_MODEL,), 0.02),
    }


def reference(x, norm_mix_g, w_in, idx_k_ln_g, idx_k_ln_b, rel_bias, ret_gn_g, w_out,
              norm_ffn_g, w_gate, w_up, w_down, norm_final_g):
    B, L, _ = x.shape
    k_top = min(TOPK_MAX, L // 4)
    pos = jnp.arange(L, dtype=jnp.int32)
    offsets = list(np.cumsum(SPLIT_SIZES)[:-1])
    for layer in range(DEPTH):
        h = rmsnorm(x, norm_mix_g[layer])
        proj = jnp.einsum('bld,de->ble', h, w_in[layer])
        (qa, ka, va, q_idx, k_idx, w_idx, qb, kb, vb, gb) = jnp.split(proj, offsets, axis=-1)
        qa = qa.reshape(B, L, N_HEADS_A, HEAD_DIM_A)
        ka = ka.reshape(B, L, N_HEADS_A, HEAD_DIM_A)
        va = va.reshape(B, L, N_HEADS_A, HEAD_DIM_A)
        q_idx = q_idx.reshape(B, L, N_IDX_HEADS, IDX_DIM)
        k_idx = layernorm(k_idx, idx_k_ln_g[layer], idx_k_ln_b[layer])
        w_idx = w_idx * (N_IDX_HEADS ** -0.5)
        out_a = sparse_attention(qa, ka, va, q_idx, k_idx, w_idx, rel_bias, k_top)
        qb = rotate(qb.reshape(B, L, N_HEADS_B, HEAD_DIM_B), pos)
        kb = rotate(kb.reshape(B, L, N_HEADS_B, HEAD_DIM_B), pos)
        vb = vb.reshape(B, L, N_HEADS_B, HEAD_DIM_B)
        gb = gb.reshape(B, L, N_HEADS_B, HEAD_DIM_B)
        out_b = retention(qb, kb, vb, gb, ret_gn_g[layer])
        mix = jnp.concatenate([out_a, out_b], axis=-1)
        x = x + jnp.einsum('ble,ed->bld', mix, w_out[layer])
        h2 = rmsnorm(x, norm_ffn_g[layer])
        u = jax.nn.silu(jnp.einsum('bld,df->blf', h2, w_gate[layer])) * jnp.einsum('bld,df->blf', h2, w_up[layer])
        x = x + jnp.einsum('blf,fd->bld', u, w_down[layer])
    return rmsnorm(x, norm_final_g)
```

```python
import functools
import math

import jax
import jax.numpy as jnp
import numpy as np
from jax import lax
from jax.experimental import pallas as pl
from jax.experimental.pallas import tpu as pltpu

N_HEADS_A = 8
HEAD_DIM_A = 64
N_IDX_HEADS = 8
IDX_DIM = 64
TOPK_MAX = 256
N_HEADS_B = 4
HEAD_DIM_B = 128
RET_THETA_BASE = 10000.0
N_BUCKETS = 32
MAX_DISTANCE = 128
EPS = 1e-6

D_A = N_HEADS_A * HEAD_DIM_A
D_B = N_HEADS_B * HEAD_DIM_B

LANES = 128
MXU_COLS = 256

PROJ_ROWS = 512
ATT_TQ = MXU_COLS
ATT_TK = LANES
RET_CHUNK = 256
FFN_ROWS = 256
VMEM_LIMIT = 56 * 1024 * 1024

NEG_BIG = -1e30
INT_MIN = -(2 ** 31)


def _resident(block_shape, index_map):
    return pl.BlockSpec(block_shape, index_map, pipeline_mode=pl.Buffered(1))


def _proj_kernel(x_ref, g_ref, wa_ref, ws_ref, wb_ref, lng_ref, lnb_ref,
                 oa_ref, okidx_ref, ow_ref, ob_ref):
    x = x_ref[...]
    h = x * lax.rsqrt(jnp.mean(x * x, axis=-1, keepdims=True) + EPS) * g_ref[...]
    hb = h.astype(jnp.bfloat16)
    oa_ref[...] = jnp.dot(hb, wa_ref[...], preferred_element_type=jnp.float32).astype(oa_ref.dtype)
    ob_ref[...] = jnp.dot(hb, wb_ref[...], preferred_element_type=jnp.float32)
    s = jnp.dot(hb, ws_ref[...], preferred_element_type=jnp.float32)
    lane = lax.broadcasted_iota(jnp.int32, s.shape, 1)
    is_k = lane < IDX_DIM
    mu = jnp.sum(jnp.where(is_k, s, 0.0), axis=-1, keepdims=True) * (1.0 / IDX_DIM)
    d = jnp.where(is_k, s - mu, 0.0)
    var = jnp.sum(d * d, axis=-1, keepdims=True) * (1.0 / IDX_DIM)
    kn = d * lax.rsqrt(var + EPS) * lng_ref[...] + lnb_ref[...]
    okidx_ref[...] = jnp.where(is_k, kn, 0.0).astype(okidx_ref.dtype)
    ow_ref[...] = s * (N_IDX_HEADS ** -0.5)


def _project(x2, g, wa, ws, wb, lng, lnb):
    rows, d_model = x2.shape
    tm = PROJ_ROWS
    na, nb = wa.shape[1], wb.shape[1]
    const = lambda i: (0, 0)
    return pl.pallas_call(
        _proj_kernel,
        grid=(rows // tm,),
        in_specs=[
            pl.BlockSpec((tm, d_model), lambda i: (i, 0)),
            _resident((1, d_model), const),
            _resident((d_model, na), const),
            _resident((d_model, LANES), const),
            _resident((d_model, nb), const),
            _resident((1, LANES), const),
            _resident((1, LANES), const),
        ],
        out_specs=[
            pl.BlockSpec((tm, na), lambda i: (i, 0)),
            pl.BlockSpec((tm, LANES), lambda i: (i, 0)),
            pl.BlockSpec((tm, LANES), lambda i: (i, 0)),
            pl.BlockSpec((tm, nb), lambda i: (i, 0)),
        ],
        out_shape=[
            jax.ShapeDtypeStruct((rows, na), jnp.bfloat16),
            jax.ShapeDtypeStruct((rows, LANES), jnp.bfloat16),
            jax.ShapeDtypeStruct((rows, LANES), jnp.float32),
            jax.ShapeDtypeStruct((rows, nb), jnp.float32),
        ],
        compiler_params=pltpu.CompilerParams(
            dimension_semantics=("arbitrary",), vmem_limit_bytes=VMEM_LIMIT),
        name="rmsnorm_in_proj",
    )(x2, g, wa, ws, wb, lng, lnb)


def _attn_kernel(kidx_ref, ka_ref, vt_ref, qit_ref, qat_ref, wt_ref, bias_ref,
                 out_ref, keys_ref, m_ref, l_ref, acc_ref):
    i = pl.program_id(1)
    tq, tk = ATT_TQ, ATT_TK
    n_chunks = (i + 1) * (tq // tk)
    t_idx = i * tq + lax.broadcasted_iota(jnp.int32, (1, tq), 1)

    def score_chunk(c, carry):
        kc = kidx_ref[0, c]
        acc = jnp.zeros((tk, tq), jnp.float32)
        for h in range(N_IDX_HEADS):
            d = jnp.dot(kc, qit_ref[0, h], preferred_element_type=jnp.float32)
            acc = acc + wt_ref[0, h:h + 1, :] * jnp.maximum(d, 0.0)
        bits = lax.bitcast_convert_type(acc + 0.0, jnp.int32)
        key = jnp.where(bits < 0, bits ^ jnp.int32(0x7FFFFFFF), bits)
        s_idx = c * tk + lax.broadcasted_iota(jnp.int32, (tk, tq), 0)
        keys_ref[c] = jnp.where(s_idx <= t_idx, key, jnp.int32(INT_MIN))
        return carry

    lax.fori_loop(0, n_chunks, score_chunk, 0)

    k_row = jnp.minimum(t_idx + 1, TOPK_MAX)

    def count_ge(cand):
        def body(c, acc):
            return acc + jnp.where(keys_ref[c] >= cand, 1, 0).astype(jnp.int32)
        acc = lax.fori_loop(0, n_chunks, body, jnp.zeros((tk, tq), jnp.int32))
        return jnp.sum(acc, axis=0, keepdims=True)

    def bit_round(r, prefix):
        cand = prefix + lax.shift_left(jnp.int32(1), 31 - r)
        return jnp.where(count_ge(cand) >= k_row, cand, prefix)

    vstar = lax.fori_loop(0, 32, bit_round, jnp.full((1, tq), INT_MIN, jnp.int32))

    m_ref[...] = jnp.full(m_ref.shape, NEG_BIG, jnp.float32)
    l_ref[...] = jnp.zeros(l_ref.shape, jnp.float32)
    acc_ref[...] = jnp.zeros(acc_ref.shape, jnp.float32)
    first_near = (tq // tk) * i - 1

    def attn_chunk(c, carry):
        maskb = jnp.where(keys_ref[c] >= vstar, 0.0, NEG_BIG)
        tbl = jnp.clip(c - first_near + 1, 0, 3)
        for h in range(N_HEADS_A):
            lo = LANES * (h // 2)
            lg = jnp.dot(ka_ref[0, c, :, lo:lo + LANES], qat_ref[0, h],
                         preferred_element_type=jnp.float32)
            lm = lg + bias_ref[tbl, h] + maskb
            m_old = m_ref[h:h + 1, :]
            m_new = jnp.maximum(m_old, jnp.max(lm, axis=0, keepdims=True))
            alpha = jnp.exp(m_old - m_new)
            p = jnp.exp(lm - m_new)
            l_ref[h:h + 1, :] = alpha * l_ref[h:h + 1, :] + jnp.sum(p, axis=0, keepdims=True)
            m_ref[h:h + 1, :] = m_new
            r0 = HEAD_DIM_A * h
            pv = jnp.dot(vt_ref[0, c, r0:r0 + HEAD_DIM_A, :], p.astype(jnp.bfloat16),
                         preferred_element_type=jnp.float32)
            acc_ref[r0:r0 + HEAD_DIM_A, :] = alpha * acc_ref[r0:r0 + HEAD_DIM_A, :] + pv
        return carry

    lax.fori_loop(0, n_chunks, attn_chunk, 0)

    for h in range(N_HEADS_A):
        r0 = HEAD_DIM_A * h
        inv = 1.0 / l_ref[h:h + 1, :]
        out_ref[0, r0:r0 + HEAD_DIM_A, :] = (acc_ref[r0:r0 + HEAD_DIM_A, :] * inv).astype(out_ref.dtype)


def _bias_bucket_tables():
    tq, tk = ATT_TQ, ATT_TK
    r = np.arange(tk)[:, None]
    j = np.arange(tq)[None, :]
    tiles = []
    for base in (None, tk, 0, -tk):
        if base is None:
            tiles.append(np.full((tk, tq), N_BUCKETS - 1, np.int32))
            continue
        n = np.maximum(base + j - r, 0)
        max_exact = N_BUCKETS // 2
        nf = np.maximum(n, max_exact).astype(np.float64)
        large = max_exact + (np.log(nf / max_exact) / math.log(MAX_DISTANCE / max_exact)
                             * (N_BUCKETS - max_exact)).astype(np.int32)
        large = np.minimum(large, N_BUCKETS - 1)
        tiles.append(np.where(n < max_exact, n, large).astype(np.int32))
    return np.stack(tiles)


def _sparse_attention(kidx4, ka4, vt4, qit, qat, wt, bias_tiles):
    b, n_c = kidx4.shape[0], kidx4.shape[1]
    seq = n_c * ATT_TK
    tq = ATT_TQ
    return pl.pallas_call(
        _attn_kernel,
        grid=(b, seq // tq),
        in_specs=[
            _resident((1, n_c, ATT_TK, LANES), lambda bi, i: (bi, 0, 0, 0)),
            _resident((1, n_c, ATT_TK, D_A), lambda bi, i: (bi, 0, 0, 0)),
            _resident((1, n_c, D_A, ATT_TK), lambda bi, i: (bi, 0, 0, 0)),
            pl.BlockSpec((1, N_IDX_HEADS, LANES, tq), lambda bi, i: (bi, 0, 0, i)),
            pl.BlockSpec((1, N_HEADS_A, LANES, tq), lambda bi, i: (bi, 0, 0, i)),
            pl.BlockSpec((1, N_IDX_HEADS, tq), lambda bi, i: (bi, 0, i)),
            _resident((4, N_HEADS_A, ATT_TK, tq), lambda bi, i: (0, 0, 0, 0)),
        ],
        out_specs=pl.BlockSpec((1, D_A, tq), lambda bi, i: (bi, 0, i)),
        out_shape=jax.ShapeDtypeStruct((b, D_A, seq), jnp.bfloat16),
        scratch_shapes=[
            pltpu.VMEM((n_c, ATT_TK, tq), jnp.int32),
            pltpu.VMEM((N_HEADS_A, tq), jnp.float32),
            pltpu.VMEM((N_HEADS_A, tq), jnp.float32),
            pltpu.VMEM((D_A, tq), jnp.float32),
        ],
        compiler_params=pltpu.CompilerParams(
            dimension_semantics=("arbitrary", "arbitrary"), vmem_limit_bytes=VMEM_LIMIT),
        name="indexer_topk_attention",
    )(kidx4, ka4, vt4, qit, qat, wt, bias_tiles)


def _ret_kernel(q_ref, k_ref, v_ref, g_ref, cos_ref, sin_ref, dec_ref, qdec_ref, kdec_ref,
                cdec_ref, gn_ref, out_ref, state_ref):
    @pl.when(pl.program_id(1) == 0)
    def _():
        state_ref[...] = jnp.zeros(state_ref.shape, jnp.float32)

    cos = cos_ref[...]
    sin = sin_ref[...]
    half = HEAD_DIM_B // 2
    for h in range(N_HEADS_B):
        lo = HEAD_DIM_B * h
        q = q_ref[0, :, lo:lo + HEAD_DIM_B]
        k = k_ref[0, :, lo:lo + HEAD_DIM_B]
        qr = q * cos + pltpu.roll(q, half, 1) * sin
        ks = (k * cos + pltpu.roll(k, half, 1) * sin) * (HEAD_DIM_B ** -0.5)
        qb = qr.astype(jnp.bfloat16)
        vb = v_ref[0, :, lo:lo + HEAD_DIM_B].astype(jnp.bfloat16)
        qk = lax.dot_general(qb, ks.astype(jnp.bfloat16), (((1,), (1,)), ((), ())),
                             preferred_element_type=jnp.float32)
        inner = (qk * dec_ref[h]).astype(jnp.bfloat16)
        state = state_ref[h]
        y = (jnp.dot(inner, vb, preferred_element_type=jnp.float32)
             + jnp.dot(qb, state.astype(jnp.bfloat16), preferred_element_type=jnp.float32) * qdec_ref[h])
        kd_t = (ks * kdec_ref[h]).T.astype(jnp.bfloat16)
        state_ref[h] = state * cdec_ref[h] + jnp.dot(kd_t, vb, preferred_element_type=jnp.float32)
        yn = y * lax.rsqrt(jnp.mean(y * y, axis=-1, keepdims=True) + EPS) * gn_ref[:, lo:lo + HEAD_DIM_B]
        g = g_ref[0, :, lo:lo + HEAD_DIM_B]
        out_ref[0, :, lo:lo + HEAD_DIM_B] = (yn * (g * jax.nn.sigmoid(g))).astype(out_ref.dtype)


def _retention_tables(seq):
    c = RET_CHUNK
    half = HEAD_DIM_B // 2
    theta = 1.0 / (RET_THETA_BASE ** jnp.linspace(0.0, 1.0, half, dtype=jnp.float32))
    ang = jnp.arange(seq, dtype=jnp.int32).astype(jnp.float32)[:, None] * theta[None, :]
    cos = jnp.concatenate([jnp.cos(ang), jnp.cos(ang)], axis=-1)
    sin = jnp.concatenate([-jnp.sin(ang), jnp.sin(ang)], axis=-1)
    gamma = 1.0 - 2.0 ** (-5.0 - jnp.arange(N_HEADS_B, dtype=jnp.float32))
    log_g = jnp.log(gamma)
    pos = jnp.arange(c, dtype=jnp.float32)
    diff = pos[:, None] - pos[None, :]
    dec = jnp.where(diff >= 0, jnp.exp(log_g[:, None, None] * jnp.maximum(diff, 0.0)), 0.0)
    qdec = jnp.exp(log_g[:, None] * (pos[None, :] + 1.0))
    kdec = jnp.exp(log_g[:, None] * (c - 1.0 - pos[None, :]))
    cdec = jnp.exp(log_g * c)
    bc = lambda a: jnp.broadcast_to(a[:, :, None], (N_HEADS_B, c, HEAD_DIM_B))
    cdec_b = jnp.broadcast_to(cdec[:, None, None], (N_HEADS_B, 1, HEAD_DIM_B))
    return cos, sin, dec, bc(qdec), bc(kdec), cdec_b


def _retention(ob3, gn):
    b, seq, _ = ob3.shape
    c = RET_CHUNK
    cos, sin, dec, qdec, kdec, cdec = _retention_tables(seq)
    col = lambda j: (lambda bi, ci: (bi, ci, j))
    const3 = lambda bi, ci: (0, 0, 0)
    return pl.pallas_call(
        _ret_kernel,
        grid=(b, seq // c),
        in_specs=[
            pl.BlockSpec((1, c, D_B), col(0)),
            pl.BlockSpec((1, c, D_B), col(1)),
            pl.BlockSpec((1, c, D_B), col(2)),
            pl.BlockSpec((1, c, D_B), col(3)),
            pl.BlockSpec((c, HEAD_DIM_B), lambda bi, ci: (ci, 0)),
            pl.BlockSpec((c, HEAD_DIM_B), lambda bi, ci: (ci, 0)),
            _resident((N_HEADS_B, c, c), const3),
            _resident((N_HEADS_B, c, HEAD_DIM_B), const3),
            _resident((N_HEADS_B, c, HEAD_DIM_B), const3),
            _resident((N_HEADS_B, 1, HEAD_DIM_B), const3),
            _resident((1, D_B), lambda bi, ci: (0, 0)),
        ],
        out_specs=pl.BlockSpec((1, c, D_B), lambda bi, ci: (bi, ci, 0)),
        out_shape=jax.ShapeDtypeStruct((b, seq, D_B), jnp.bfloat16),
        scratch_shapes=[pltpu.VMEM((N_HEADS_B, HEAD_DIM_B, HEAD_DIM_B), jnp.float32)],
        compiler_params=pltpu.CompilerParams(
            dimension_semantics=("arbitrary", "arbitrary"), vmem_limit_bytes=VMEM_LIMIT),
        name="retention",
    )(ob3, ob3, ob3, ob3, cos, sin, dec, qdec, kdec, cdec, gn)


def _ffn_kernel(x_ref, a_ref, b_ref, woa_ref, wob_ref, gffn_ref, wg_ref, wu_ref, wd_ref, gfin_ref,
                out_ref):
    x1 = (x_ref[...]
          + jnp.dot(a_ref[...], woa_ref[...], preferred_element_type=jnp.float32)
          + jnp.dot(b_ref[...], wob_ref[...], preferred_element_type=jnp.float32))
    h2 = (x1 * lax.rsqrt(jnp.mean(x1 * x1, axis=-1, keepdims=True) + EPS) * gffn_ref[...]).astype(jnp.bfloat16)
    gate = jnp.dot(h2, wg_ref[...], preferred_element_type=jnp.float32)
    up = jnp.dot(h2, wu_ref[...], preferred_element_type=jnp.float32)
    u = (gate * jax.nn.sigmoid(gate) * up).astype(jnp.bfloat16)
    x2 = x1 + jnp.dot(u, wd_ref[...], preferred_element_type=jnp.float32)
    out_ref[...] = x2 * lax.rsqrt(jnp.mean(x2 * x2, axis=-1, keepdims=True) + EPS) * gfin_ref[...]


def _out_ffn(x2d, a2d, b2d, woa, wob, gffn, wg, wu, wd, gfin):
    rows, d_model = x2d.shape
    d_ff = wg.shape[1]
    tm = FFN_ROWS
    const = lambda i: (0, 0)
    return pl.pallas_call(
        _ffn_kernel,
        grid=(rows // tm,),
        in_specs=[
            pl.BlockSpec((tm, d_model), lambda i: (i, 0)),
            pl.BlockSpec((tm, D_A), lambda i: (i, 0)),
            pl.BlockSpec((tm, D_B), lambda i: (i, 0)),
            _resident((D_A, d_model), const),
            _resident((D_B, d_model), const),
            _resident((1, d_model), const),
            _resident((d_model, d_ff), const),
            _resident((d_model, d_ff), const),
            _resident((d_ff, d_model), const),
            _resident((1, d_model), const),
        ],
        out_specs=pl.BlockSpec((tm, d_model), lambda i: (i, 0)),
        out_shape=jax.ShapeDtypeStruct((rows, d_model), jnp.float32),
        compiler_params=pltpu.CompilerParams(
            dimension_semantics=("arbitrary",), vmem_limit_bytes=VMEM_LIMIT),
        name="out_proj_swiglu",
    )(x2d, a2d, b2d, woa, wob, gffn, wg, wu, wd, gfin)


def kernel(x, norm_mix_g, w_in, idx_k_ln_g, idx_k_ln_b, rel_bias, ret_gn_g, w_out, norm_ffn_g,
           w_gate, w_up, w_down, norm_final_g):
    b, seq, d_model = x.shape
    assert seq % ATT_TQ == 0 and seq % RET_CHUNK == 0 and (b * seq) % PROJ_ROWS == 0
    assert norm_mix_g.shape[0] == 1, "single-layer block"
    bf = jnp.bfloat16
    rows = b * seq
    n_c = seq // ATT_TK
    o = 0
    w = w_in[0]
    qa_w = w[:, o:o + D_A] * (HEAD_DIM_A ** -0.5); o += D_A
    ka_w = w[:, o:o + D_A]; o += D_A
    va_w = w[:, o:o + D_A]; o += D_A
    qi_w = w[:, o:o + N_IDX_HEADS * IDX_DIM] * (IDX_DIM ** -0.5); o += N_IDX_HEADS * IDX_DIM
    small_w = w[:, o:o + IDX_DIM + N_IDX_HEADS]; o += IDX_DIM + N_IDX_HEADS
    b_w = w[:, o:]
    wa = jnp.concatenate([qa_w, ka_w, va_w, qi_w], axis=1).astype(bf)
    ws = jnp.pad(small_w, ((0, 0), (0, LANES - small_w.shape[1]))).astype(bf)
    wb = b_w.astype(bf)
    lng = jnp.pad(idx_k_ln_g[0], (0, LANES - IDX_DIM)).reshape(1, LANES)
    lnb = jnp.pad(idx_k_ln_b[0], (0, LANES - IDX_DIM)).reshape(1, LANES)

    x2d = x.reshape(rows, d_model)
    oa, kidx, wsm, ob = _project(x2d, norm_mix_g[0].reshape(1, d_model), wa, ws, wb, lng, lnb)

    oa3 = oa.reshape(b, seq, 4 * D_A)
    qa = oa3[:, :, 0:D_A].reshape(b, seq, N_HEADS_A, HEAD_DIM_A)
    ka4 = oa3[:, :, D_A:2 * D_A].reshape(b, n_c, ATT_TK, D_A)
    vt4 = jnp.swapaxes(oa3[:, :, 2 * D_A:3 * D_A].reshape(b, n_c, ATT_TK, D_A), 2, 3)
    qi = oa3[:, :, 3 * D_A:].reshape(b, seq, N_IDX_HEADS, IDX_DIM)
    qit = jnp.pad(jnp.transpose(qi, (0, 2, 3, 1)), ((0, 0), (0, 0), (0, LANES - IDX_DIM), (0, 0)))
    qat_half = jnp.transpose(qa, (0, 2, 3, 1))
    zeros = jnp.zeros_like(qat_half)
    odd = (jnp.arange(N_HEADS_A) % 2 == 1)[None, :, None, None]
    qat = jnp.concatenate([jnp.where(odd, zeros, qat_half), jnp.where(odd, qat_half, zeros)], axis=2)
    wt = jnp.transpose(wsm.reshape(b, seq, LANES)[:, :, IDX_DIM:IDX_DIM + N_IDX_HEADS], (0, 2, 1))
    kidx4 = kidx.reshape(b, n_c, ATT_TK, LANES)
    bias_tiles = jnp.transpose(rel_bias[_bias_bucket_tables()], (0, 3, 1, 2))

    out_a_t = _sparse_attention(kidx4, ka4, vt4, qit, qat, wt, bias_tiles)
    out_a = jnp.swapaxes(out_a_t, 1, 2).reshape(rows, D_A)

    out_b = _retention(ob.reshape(b, seq, 4 * D_B), ret_gn_g[0].reshape(1, D_B)).reshape(rows, D_B)

    wo = w_out[0]
    out = _out_ffn(x2d, out_a, out_b, wo[:D_A].astype(bf), wo[D_A:].astype(bf),
                   norm_ffn_g[0].reshape(1, d_model), w_gate[0].astype(bf), w_up[0].astype(bf),
                   w_down[0].astype(bf), norm_final_g.reshape(1, d_model))
    return out.reshape(b, seq, d_model)
```

```python
import math

import jax
import jax.numpy as jnp
import numpy as np
from jax import lax
from jax.experimental import pallas as pl
from jax.experimental.pallas import tpu as pltpu

N_HEADS_A = 8
HEAD_DIM_A = 64
N_IDX_HEADS = 8
IDX_DIM = 64
TOPK_MAX = 256
N_HEADS_B = 4
HEAD_DIM_B = 128
RET_THETA_BASE = 10000.0
N_BUCKETS = 32
MAX_DISTANCE = 128
EPS = 1e-6

D_A = N_HEADS_A * HEAD_DIM_A
D_B = N_HEADS_B * HEAD_DIM_B

LANES = 128
MXU_COLS = 256

PROJ_ROWS = 512
ATT_TQ = MXU_COLS
ATT_TK = LANES
RET_CHUNK = 256
FFN_ROWS = 256
VMEM_LIMIT = 56 * 1024 * 1024

NEG_BIG = -1e30
INT_MIN = -(2 ** 31)


def _resident(block_shape, index_map):
    return pl.BlockSpec(block_shape, index_map, pipeline_mode=pl.Buffered(1))


def _proj_kernel(x_ref, g_ref, wa_ref, ws_ref, wb_ref, lng_ref, lnb_ref,
                 oa_ref, okidx_ref, ow_ref, ob_ref):
    x = x_ref[...]
    h = x * lax.rsqrt(jnp.mean(x * x, axis=-1, keepdims=True) + EPS) * g_ref[...]
    hb = h.astype(jnp.bfloat16)
    oa_ref[...] = jnp.dot(hb, wa_ref[...], preferred_element_type=jnp.float32).astype(oa_ref.dtype)
    ob_ref[...] = jnp.dot(hb, wb_ref[...], preferred_element_type=jnp.float32)
    s = jnp.dot(hb, ws_ref[...], preferred_element_type=jnp.float32)
    lane = lax.broadcasted_iota(jnp.int32, s.shape, 1)
    is_k = lane < IDX_DIM
    mu = jnp.sum(jnp.where(is_k, s, 0.0), axis=-1, keepdims=True) * (1.0 / IDX_DIM)
    d = jnp.where(is_k, s - mu, 0.0)
    var = jnp.sum(d * d, axis=-1, keepdims=True) * (1.0 / IDX_DIM)
    kn = d * lax.rsqrt(var + EPS) * lng_ref[...] + lnb_ref[...]
    okidx_ref[...] = jnp.where(is_k, kn, 0.0).astype(okidx_ref.dtype)
    ow_ref[...] = s * (N_IDX_HEADS ** -0.5)


def _project(x2, g, wa, ws, wb, lng, lnb):
    rows, d_model = x2.shape
    tm = PROJ_ROWS
    na, nb = wa.shape[1], wb.shape[1]
    const = lambda i: (0, 0)
    return pl.pallas_call(
        _proj_kernel,
        grid=(rows // tm,),
        in_specs=[
            pl.BlockSpec((tm, d_model), lambda i: (i, 0)),
            _resident((1, d_model), const),
            _resident((d_model, na), const),
            _resident((d_model, LANES), const),
            _resident((d_model, nb), const),
            _resident((1, LANES), const),
            _resident((1, LANES), const),
        ],
        out_specs=[
            pl.BlockSpec((tm, na), lambda i: (i, 0)),
            pl.BlockSpec((tm, LANES), lambda i: (i, 0)),
            pl.BlockSpec((tm, LANES), lambda i: (i, 0)),
            pl.BlockSpec((tm, nb), lambda i: (i, 0)),
        ],
        out_shape=[
            jax.ShapeDtypeStruct((rows, na), jnp.bfloat16),
            jax.ShapeDtypeStruct((rows, LANES), jnp.bfloat16),
            jax.ShapeDtypeStruct((rows, LANES), jnp.float32),
            jax.ShapeDtypeStruct((rows, nb), jnp.float32),
        ],
        compiler_params=pltpu.CompilerParams(
            dimension_semantics=("arbitrary",), vmem_limit_bytes=VMEM_LIMIT),
        name="rmsnorm_in_proj",
    )(x2, g, wa, ws, wb, lng, lnb)


def _attn_kernel(kidx_ref, ka_ref, vt_ref, qit_ref, qat_ref, wt_ref, bias_ref, out_ref,
                 keys_ref, lg_ref, p_ref, alpha_ref, m_ref, l_ref, acc_ref, *, index_bits):
    i = pl.program_id(1)
    tq, tk = ATT_TQ, ATT_TK
    n_pairs = N_HEADS_A // 2
    n_chunks = (i + 1) * (tq // tk)
    dummy_chunk = keys_ref.shape[0] - 1
    t_idx = i * tq + lax.broadcasted_iota(jnp.int32, (1, tq), 1)

    def score_chunk(c, carry):
        kc = kidx_ref[0, c]
        acc = jnp.zeros((tk, tq), jnp.float32)
        for j in range(N_IDX_HEADS // 2):
            d = jnp.dot(kc, qit_ref[0, 0, j], preferred_element_type=jnp.float32)
            acc = acc + wt_ref[0, 2 * j:2 * j + 1, :] * jnp.maximum(d[:, :tq], 0.0)
            acc = acc + wt_ref[0, 2 * j + 1:2 * j + 2, :] * jnp.maximum(d[:, tq:], 0.0)
        bits = lax.bitcast_convert_type(acc + 0.0, jnp.int32)
        key = jnp.where(bits < 0, bits ^ jnp.int32(0x7FFFFFFF), bits)
        s_idx = c * tk + lax.broadcasted_iota(jnp.int32, (tk, tq), 0)
        keys_ref[c] = jnp.where(s_idx <= t_idx, key, jnp.int32(INT_MIN))
        return carry

    lax.fori_loop(0, n_chunks, score_chunk, 0)

    k_row = jnp.minimum(t_idx + 1, TOPK_MAX)

    def count(pred):
        def body(c, acc):
            return acc + jnp.where(pred(c, keys_ref[c]), 1, 0).astype(jnp.int32)
        acc = lax.fori_loop(0, n_chunks, body, jnp.zeros((tk, tq), jnp.int32))
        return jnp.sum(acc, axis=0, keepdims=True)

    def bit_round(r, prefix):
        cand = prefix + lax.shift_left(jnp.int32(1), 31 - r)
        return jnp.where(count(lambda c, k: k >= cand) >= k_row, cand, prefix)

    vstar = lax.fori_loop(0, 32, bit_round, jnp.full((1, tq), INT_MIN, jnp.int32))

    n_ge = count(lambda c, k: k >= vstar)

    @pl.when(jnp.max(n_ge - k_row) > 0)
    def _():
        need = k_row - count(lambda c, k: k > vstar)

        def idx_round(r, x):
            cand = x + lax.shift_left(jnp.int32(1), index_bits - 1 - r)

            def tie_below(c, k):
                s_idx = c * tk + lax.broadcasted_iota(jnp.int32, (tk, tq), 0)
                return (k == vstar) & (s_idx < cand)
            return jnp.where(count(tie_below) < need, cand, x)

        last = lax.fori_loop(0, index_bits, idx_round, jnp.zeros((1, tq), jnp.int32))

        def drop(c, carry):
            k = keys_ref[c]
            s_idx = c * tk + lax.broadcasted_iota(jnp.int32, (tk, tq), 0)
            keys_ref[c] = jnp.where((k == vstar) & (s_idx > last), vstar - 1, k)
            return carry
        lax.fori_loop(0, n_chunks, drop, 0)

    def to_mask(c, carry):
        maskb = jnp.where(keys_ref[c] >= vstar, 0.0, NEG_BIG)
        keys_ref[c] = lax.bitcast_convert_type(maskb, jnp.int32)
        return carry
    lax.fori_loop(0, n_chunks, to_mask, 0)
    keys_ref[dummy_chunk] = lax.bitcast_convert_type(jnp.full((tk, tq), NEG_BIG, jnp.float32), jnp.int32)

    m_ref[...] = jnp.full(m_ref.shape, NEG_BIG, jnp.float32)
    l_ref[...] = jnp.zeros(l_ref.shape, jnp.float32)
    acc_ref[...] = jnp.zeros(acc_ref.shape, jnp.float32)
    lg_ref[...] = jnp.zeros(lg_ref.shape, jnp.float32)
    p_ref[...] = jnp.zeros(p_ref.shape, p_ref.dtype)
    alpha_ref[...] = jnp.ones(alpha_ref.shape, jnp.float32)
    first_near = (tq // tk) * i - 1

    def pv_stage(c, buf):
        for h in range(N_HEADS_A):
            r0 = HEAD_DIM_A * h
            pv = jnp.dot(vt_ref[0, c, r0:r0 + HEAD_DIM_A, :], p_ref[buf, h],
                         preferred_element_type=jnp.float32)
            acc_ref[r0:r0 + HEAD_DIM_A, :] = alpha_ref[buf, h:h + 1, :] * acc_ref[r0:r0 + HEAD_DIM_A, :] + pv

    def qk_stage(c, buf):
        for j in range(n_pairs):
            d = jnp.dot(ka_ref[0, c, :, LANES * j:LANES * (j + 1)], qat_ref[0, 0, j],
                        preferred_element_type=jnp.float32)
            lg_ref[buf, 2 * j] = d[:, :tq]
            lg_ref[buf, 2 * j + 1] = d[:, tq:]

    def softmax_stage(c, valid, buf):
        cm = jnp.where(valid, c, dummy_chunk)
        tbl = jnp.where(valid, jnp.clip(c - first_near + 1, 0, 3), 0)
        maskb = lax.bitcast_convert_type(keys_ref[cm], jnp.float32)
        for h in range(N_HEADS_A):
            lm = lg_ref[buf, h] + bias_ref[tbl, h] + maskb
            m_old = m_ref[h:h + 1, :]
            m_new = jnp.maximum(m_old, jnp.max(lm, axis=0, keepdims=True))
            alpha = jnp.exp(m_old - m_new)
            p = jnp.exp(lm - m_new)
            l_ref[h:h + 1, :] = alpha * l_ref[h:h + 1, :] + jnp.sum(p, axis=0, keepdims=True)
            m_ref[h:h + 1, :] = m_new
            alpha_ref[buf, h:h + 1, :] = alpha
            p_ref[buf, h] = p.astype(p_ref.dtype)

    def step(s, buf):
        pv_stage(jnp.maximum(s - 2, 0), buf)
        qk_stage(jnp.minimum(s, n_chunks - 1), buf)
        sm = s - 1
        softmax_stage(jnp.clip(sm, 0, n_chunks - 1), (sm >= 0) & (sm < n_chunks), 1 - buf)

    def two_steps(t, carry):
        step(2 * t, 0)
        step(2 * t + 1, 1)
        return carry

    lax.fori_loop(0, (n_chunks + 2) // 2, two_steps, 0)

    for h in range(N_HEADS_A):
        r0 = HEAD_DIM_A * h
        inv = 1.0 / l_ref[h:h + 1, :]
        out_ref[0, r0:r0 + HEAD_DIM_A, :] = (acc_ref[r0:r0 + HEAD_DIM_A, :] * inv).astype(out_ref.dtype)


def _bias_tiles(rel_bias):
    tq, tk = ATT_TQ, ATT_TK
    lo = -(tq - 1)
    dist = np.arange(lo, tk + tq)
    n = np.maximum(dist, 0)
    max_exact = N_BUCKETS // 2
    nf = np.maximum(n, max_exact).astype(np.float64)
    large = max_exact + (np.log(nf / max_exact) / math.log(MAX_DISTANCE / max_exact)
                         * (N_BUCKETS - max_exact)).astype(np.int32)
    bucket = np.where(n < max_exact, n, np.minimum(large, N_BUCKETS - 1)).astype(np.int32)
    vec = (rel_bias[bucket] - rel_bias[N_BUCKETS - 1][None, :]).T
    tiles = [jnp.zeros((N_HEADS_A, tk, tq), rel_bias.dtype)]
    for base in (tk, 0, -tk):
        rows = [lax.slice_in_dim(vec, base - r - lo, base - r - lo + tq, axis=1) for r in range(tk)]
        tiles.append(jnp.stack(rows, axis=1))
    return jnp.stack(tiles)


def _sparse_attention(kidx4, ka4, vt4, qit2, qat2, wt, bias_tiles):
    b, n_c = kidx4.shape[0], kidx4.shape[1]
    seq = n_c * ATT_TK
    tq = ATT_TQ
    n_pairs = N_HEADS_A // 2
    kern = lambda *refs: _attn_kernel(*refs, index_bits=max(1, (seq - 1).bit_length()))
    return pl.pallas_call(
        kern,
        grid=(b, seq // tq),
        in_specs=[
            _resident((1, n_c, ATT_TK, LANES), lambda bi, i: (bi, 0, 0, 0)),
            _resident((1, n_c, ATT_TK, D_A), lambda bi, i: (bi, 0, 0, 0)),
            _resident((1, n_c, D_A, ATT_TK), lambda bi, i: (bi, 0, 0, 0)),
            pl.BlockSpec((1, 1, N_IDX_HEADS // 2, LANES, 2 * tq), lambda bi, i: (bi, i, 0, 0, 0)),
            pl.BlockSpec((1, 1, n_pairs, LANES, 2 * tq), lambda bi, i: (bi, i, 0, 0, 0)),
            pl.BlockSpec((1, N_IDX_HEADS, tq), lambda bi, i: (bi, 0, i)),
            _resident((4, N_HEADS_A, ATT_TK, tq), lambda bi, i: (0, 0, 0, 0)),
        ],
        out_specs=pl.BlockSpec((1, D_A, tq), lambda bi, i: (bi, 0, i)),
        out_shape=jax.ShapeDtypeStruct((b, D_A, seq), jnp.bfloat16),
        scratch_shapes=[
            pltpu.VMEM((n_c + 1, ATT_TK, tq), jnp.int32),
            pltpu.VMEM((2, N_HEADS_A, ATT_TK, tq), jnp.float32),
            pltpu.VMEM((2, N_HEADS_A, ATT_TK, tq), jnp.bfloat16),
            pltpu.VMEM((2, N_HEADS_A, tq), jnp.float32),
            pltpu.VMEM((N_HEADS_A, tq), jnp.float32),
            pltpu.VMEM((N_HEADS_A, tq), jnp.float32),
            pltpu.VMEM((D_A, tq), jnp.float32),
        ],
        compiler_params=pltpu.CompilerParams(
            dimension_semantics=("arbitrary", "arbitrary"), vmem_limit_bytes=VMEM_LIMIT),
        name="indexer_topk_attention",
    )(kidx4, ka4, vt4, qit2, qat2, wt, bias_tiles)


def _pair_lanes(xt, seq):
    b, nh = xt.shape[0], xt.shape[1]
    tq = ATT_TQ
    x = xt.reshape(b, nh // 2, 2, LANES, seq // tq, tq)
    return jnp.transpose(x, (0, 4, 1, 3, 2, 5)).reshape(b, seq // tq, nh // 2, LANES, 2 * tq)


def _ret_kernel(q_ref, k_ref, v_ref, g_ref, cos_ref, sin_ref, dec_ref, qdec_ref, kdec_ref,
                cdec_ref, gn_ref, out_ref, state_ref):
    @pl.when(pl.program_id(1) == 0)
    def _():
        state_ref[...] = jnp.zeros(state_ref.shape, jnp.float32)

    cos = cos_ref[...]
    sin = sin_ref[...]
    half = HEAD_DIM_B // 2
    for h in range(N_HEADS_B):
        lo = HEAD_DIM_B * h
        q = q_ref[0, :, lo:lo + HEAD_DIM_B]
        k = k_ref[0, :, lo:lo + HEAD_DIM_B]
        qr = q * cos + pltpu.roll(q, half, 1) * sin
        ks = (k * cos + pltpu.roll(k, half, 1) * sin) * (HEAD_DIM_B ** -0.5)
        qb = qr.astype(jnp.bfloat16)
        vb = v_ref[0, :, lo:lo + HEAD_DIM_B].astype(jnp.bfloat16)
        qk = lax.dot_general(qb, ks.astype(jnp.bfloat16), (((1,), (1,)), ((), ())),
                             preferred_element_type=jnp.float32)
        inner = (qk * dec_ref[h]).astype(jnp.bfloat16)
        state = state_ref[h]
        y = (jnp.dot(inner, vb, preferred_element_type=jnp.float32)
             + jnp.dot(qb, state.astype(jnp.bfloat16), preferred_element_type=jnp.float32) * qdec_ref[h])
        kd_t = (ks * kdec_ref[h]).T.astype(jnp.bfloat16)
        state_ref[h] = state * cdec_ref[h] + jnp.dot(kd_t, vb, preferred_element_type=jnp.float32)
        yn = y * lax.rsqrt(jnp.mean(y * y, axis=-1, keepdims=True) + EPS) * gn_ref[:, lo:lo + HEAD_DIM_B]
        g = g_ref[0, :, lo:lo + HEAD_DIM_B]
        out_ref[0, :, lo:lo + HEAD_DIM_B] = (yn * (g * jax.nn.sigmoid(g))).astype(out_ref.dtype)


def _retention_tables(seq):
    c = RET_CHUNK
    half = HEAD_DIM_B // 2
    theta = 1.0 / (RET_THETA_BASE ** jnp.linspace(0.0, 1.0, half, dtype=jnp.float32))
    ang = jnp.arange(seq, dtype=jnp.int32).astype(jnp.float32)[:, None] * theta[None, :]
    cos = jnp.concatenate([jnp.cos(ang), jnp.cos(ang)], axis=-1)
    sin = jnp.concatenate([-jnp.sin(ang), jnp.sin(ang)], axis=-1)
    gamma = 1.0 - 2.0 ** (-5.0 - jnp.arange(N_HEADS_B, dtype=jnp.float32))
    log_g = jnp.log(gamma)
    pos = jnp.arange(c, dtype=jnp.float32)
    diff = pos[:, None] - pos[None, :]
    dec = jnp.where(diff >= 0, jnp.exp(log_g[:, None, None] * jnp.maximum(diff, 0.0)), 0.0)
    qdec = jnp.exp(log_g[:, None] * (pos[None, :] + 1.0))
    kdec = jnp.exp(log_g[:, None] * (c - 1.0 - pos[None, :]))
    cdec = jnp.exp(log_g * c)
    bc = lambda a: jnp.broadcast_to(a[:, :, None], (N_HEADS_B, c, HEAD_DIM_B))
    cdec_b = jnp.broadcast_to(cdec[:, None, None], (N_HEADS_B, 1, HEAD_DIM_B))
    return cos, sin, dec, bc(qdec), bc(kdec), cdec_b


def _retention(ob3, gn):
    b, seq, _ = ob3.shape
    c = RET_CHUNK
    cos, sin, dec, qdec, kdec, cdec = _retention_tables(seq)
    col = lambda j: (lambda bi, ci: (bi, ci, j))
    const3 = lambda bi, ci: (0, 0, 0)
    return pl.pallas_call(
        _ret_kernel,
        grid=(b, seq // c),
        in_specs=[
            pl.BlockSpec((1, c, D_B), col(0)),
            pl.BlockSpec((1, c, D_B), col(1)),
            pl.BlockSpec((1, c, D_B), col(2)),
            pl.BlockSpec((1, c, D_B), col(3)),
            pl.BlockSpec((c, HEAD_DIM_B), lambda bi, ci: (ci, 0)),
            pl.BlockSpec((c, HEAD_DIM_B), lambda bi, ci: (ci, 0)),
            _resident((N_HEADS_B, c, c), const3),
            _resident((N_HEADS_B, c, HEAD_DIM_B), const3),
            _resident((N_HEADS_B, c, HEAD_DIM_B), const3),
            _resident((N_HEADS_B, 1, HEAD_DIM_B), const3),
            _resident((1, D_B), lambda bi, ci: (0, 0)),
        ],
        out_specs=pl.BlockSpec((1, c, D_B), lambda bi, ci: (bi, ci, 0)),
        out_shape=jax.ShapeDtypeStruct((b, seq, D_B), jnp.bfloat16),
        scratch_shapes=[pltpu.VMEM((N_HEADS_B, HEAD_DIM_B, HEAD_DIM_B), jnp.float32)],
        compiler_params=pltpu.CompilerParams(
            dimension_semantics=("arbitrary", "arbitrary"), vmem_limit_bytes=VMEM_LIMIT),
        name="retention",
    )(ob3, ob3, ob3, ob3, cos, sin, dec, qdec, kdec, cdec, gn)


def _ffn_kernel(x_ref, a_ref, b_ref, woa_ref, wob_ref, gffn_ref, wg_ref, wu_ref, wd_ref, gfin_ref,
                out_ref):
    x1 = (x_ref[...]
          + jnp.dot(a_ref[...], woa_ref[...], preferred_element_type=jnp.float32)
          + jnp.dot(b_ref[...], wob_ref[...], preferred_element_type=jnp.float32))
    h2 = (x1 * lax.rsqrt(jnp.mean(x1 * x1, axis=-1, keepdims=True) + EPS) * gffn_ref[...]).astype(jnp.bfloat16)
    gate = jnp.dot(h2, wg_ref[...], preferred_element_type=jnp.float32)
    up = jnp.dot(h2, wu_ref[...], preferred_element_type=jnp.float32)
    u = (gate * jax.nn.sigmoid(gate) * up).astype(jnp.bfloat16)
    x2 = x1 + jnp.dot(u, wd_ref[...], preferred_element_type=jnp.float32)
    out_ref[...] = x2 * lax.rsqrt(jnp.mean(x2 * x2, axis=-1, keepdims=True) + EPS) * gfin_ref[...]


def _out_ffn(x2d, a2d, b2d, woa, wob, gffn, wg, wu, wd, gfin):
    rows, d_model = x2d.shape
    d_ff = wg.shape[1]
    tm = FFN_ROWS
    const = lambda i: (0, 0)
    return pl.pallas_call(
        _ffn_kernel,
        grid=(rows // tm,),
        in_specs=[
            pl.BlockSpec((tm, d_model), lambda i: (i, 0)),
            pl.BlockSpec((tm, D_A), lambda i: (i, 0)),
            pl.BlockSpec((tm, D_B), lambda i: (i, 0)),
            _resident((D_A, d_model), const),
            _resident((D_B, d_model), const),
            _resident((1, d_model), const),
            _resident((d_model, d_ff), const),
            _resident((d_model, d_ff), const),
            _resident((d_ff, d_model), const),
            _resident((1, d_model), const),
        ],
        out_specs=pl.BlockSpec((tm, d_model), lambda i: (i, 0)),
        out_shape=jax.ShapeDtypeStruct((rows, d_model), jnp.float32),
        compiler_params=pltpu.CompilerParams(
            dimension_semantics=("arbitrary",), vmem_limit_bytes=VMEM_LIMIT),
        name="out_proj_swiglu",
    )(x2d, a2d, b2d, woa, wob, gffn, wg, wu, wd, gfin)


def kernel(x, norm_mix_g, w_in, idx_k_ln_g, idx_k_ln_b, rel_bias, ret_gn_g, w_out, norm_ffn_g,
           w_gate, w_up, w_down, norm_final_g):
    b, seq, d_model = x.shape
    assert seq % ATT_TQ == 0 and seq % RET_CHUNK == 0 and (b * seq) % PROJ_ROWS == 0
    assert norm_mix_g.shape[0] == 1, "single-layer block"
    bf = jnp.bfloat16
    rows = b * seq
    n_c = seq // ATT_TK
    o = 0
    w = w_in[0]
    qa_w = w[:, o:o + D_A] * (HEAD_DIM_A ** -0.5); o += D_A
    ka_w = w[:, o:o + D_A]; o += D_A
    va_w = w[:, o:o + D_A]; o += D_A
    qi_w = w[:, o:o + N_IDX_HEADS * IDX_DIM] * (IDX_DIM ** -0.5); o += N_IDX_HEADS * IDX_DIM
    small_w = w[:, o:o + IDX_DIM + N_IDX_HEADS]; o += IDX_DIM + N_IDX_HEADS
    b_w = w[:, o:]
    wa = jnp.concatenate([qa_w, ka_w, va_w, qi_w], axis=1).astype(bf)
    ws = jnp.pad(small_w, ((0, 0), (0, LANES - small_w.shape[1]))).astype(bf)
    wb = b_w.astype(bf)
    lng = jnp.pad(idx_k_ln_g[0], (0, LANES - IDX_DIM)).reshape(1, LANES)
    lnb = jnp.pad(idx_k_ln_b[0], (0, LANES - IDX_DIM)).reshape(1, LANES)

    x2d = x.reshape(rows, d_model)
    oa, kidx, wsm, ob = _project(x2d, norm_mix_g[0].reshape(1, d_model), wa, ws, wb, lng, lnb)

    oa3 = oa.reshape(b, seq, 4 * D_A)
    qa = oa3[:, :, 0:D_A].reshape(b, seq, N_HEADS_A, HEAD_DIM_A)
    ka4 = oa3[:, :, D_A:2 * D_A].reshape(b, n_c, ATT_TK, D_A)
    vt4 = jnp.swapaxes(oa3[:, :, 2 * D_A:3 * D_A].reshape(b, n_c, ATT_TK, D_A), 2, 3)
    qi = oa3[:, :, 3 * D_A:].reshape(b, seq, N_IDX_HEADS, IDX_DIM)
    qit = jnp.pad(jnp.transpose(qi, (0, 2, 3, 1)), ((0, 0), (0, 0), (0, LANES - IDX_DIM), (0, 0)))
    qat_half = jnp.transpose(qa, (0, 2, 3, 1))
    zeros = jnp.zeros_like(qat_half)
    odd = (jnp.arange(N_HEADS_A) % 2 == 1)[None, :, None, None]
    qat = jnp.concatenate([jnp.where(odd, zeros, qat_half), jnp.where(odd, qat_half, zeros)], axis=2)
    wt = jnp.transpose(wsm.reshape(b, seq, LANES)[:, :, IDX_DIM:IDX_DIM + N_IDX_HEADS], (0, 2, 1))
    kidx4 = kidx.reshape(b, n_c, ATT_TK, LANES)

    out_a_t = _sparse_attention(kidx4, ka4, vt4, _pair_lanes(qit, seq), _pair_lanes(qat, seq), wt,
                                _bias_tiles(rel_bias))
    out_a = jnp.swapaxes(out_a_t, 1, 2).reshape(rows, D_A)

    out_b = _retention(ob.reshape(b, seq, 4 * D_B), ret_gn_g[0].reshape(1, D_B)).reshape(rows, D_B)

    wo = w_out[0]
    out = _out_ffn(x2d, out_a, out_b, wo[:D_A].astype(bf), wo[D_A:].astype(bf),
                   norm_ffn_g[0].reshape(1, d_model), w_gate[0].astype(bf), w_up[0].astype(bf),
                   w_down[0].astype(bf), norm_final_g.reshape(1, d_model))
    return out.reshape(b, seq, d_model)
```

```python
import math

import jax
import jax.numpy as jnp
import numpy as np
from jax import lax
from jax.experimental import pallas as pl
from jax.experimental.pallas import tpu as pltpu

N_HEADS_A = 8
HEAD_DIM_A = 64
N_IDX_HEADS = 8
IDX_DIM = 64
TOPK_MAX = 256
N_HEADS_B = 4
HEAD_DIM_B = 128
RET_THETA_BASE = 10000.0
N_BUCKETS = 32
MAX_DISTANCE = 128
EPS = 1e-6

D_A = N_HEADS_A * HEAD_DIM_A
D_B = N_HEADS_B * HEAD_DIM_B

LANES = 128
MXU_COLS = 256

PROJ_ROWS = 512
ATT_TQ = MXU_COLS
ATT_TK = LANES
RET_CHUNK = 256
FFN_ROWS = 256
VMEM_LIMIT = 56 * 1024 * 1024

NEG_BIG = -1e30
INT_MIN = -(2 ** 31)


def _resident(block_shape, index_map):
    return pl.BlockSpec(block_shape, index_map, pipeline_mode=pl.Buffered(1))


def _proj_kernel(x_ref, g_ref, wa_ref, ws_ref, wb_ref, lng_ref, lnb_ref,
                 oa_ref, okidx_ref, ow_ref, ob_ref):
    x = x_ref[...]
    h = x * lax.rsqrt(jnp.mean(x * x, axis=-1, keepdims=True) + EPS) * g_ref[...]
    hb = h.astype(jnp.bfloat16)
    oa_ref[...] = jnp.dot(hb, wa_ref[...], preferred_element_type=jnp.float32).astype(oa_ref.dtype)
    ob_ref[...] = jnp.dot(hb, wb_ref[...], preferred_element_type=jnp.float32)
    s = jnp.dot(hb, ws_ref[...], preferred_element_type=jnp.float32)
    lane = lax.broadcasted_iota(jnp.int32, s.shape, 1)
    is_k = lane < IDX_DIM
    mu = jnp.sum(jnp.where(is_k, s, 0.0), axis=-1, keepdims=True) * (1.0 / IDX_DIM)
    d = jnp.where(is_k, s - mu, 0.0)
    var = jnp.sum(d * d, axis=-1, keepdims=True) * (1.0 / IDX_DIM)
    kn = d * lax.rsqrt(var + EPS) * lng_ref[...] + lnb_ref[...]
    okidx_ref[...] = jnp.where(is_k, kn, 0.0).astype(okidx_ref.dtype)
    ow_ref[...] = s * (N_IDX_HEADS ** -0.5)


def _project(x2, g, wa, ws, wb, lng, lnb):
    rows, d_model = x2.shape
    tm = PROJ_ROWS
    na, nb = wa.shape[1], wb.shape[1]
    const = lambda i: (0, 0)
    return pl.pallas_call(
        _proj_kernel,
        grid=(rows // tm,),
        in_specs=[
            pl.BlockSpec((tm, d_model), lambda i: (i, 0)),
            _resident((1, d_model), const),
            _resident((d_model, na), const),
            _resident((d_model, LANES), const),
            _resident((d_model, nb), const),
            _resident((1, LANES), const),
            _resident((1, LANES), const),
        ],
        out_specs=[
            pl.BlockSpec((tm, na), lambda i: (i, 0)),
            pl.BlockSpec((tm, LANES), lambda i: (i, 0)),
            pl.BlockSpec((tm, LANES), lambda i: (i, 0)),
            pl.BlockSpec((tm, nb), lambda i: (i, 0)),
        ],
        out_shape=[
            jax.ShapeDtypeStruct((rows, na), jnp.bfloat16),
            jax.ShapeDtypeStruct((rows, LANES), jnp.bfloat16),
            jax.ShapeDtypeStruct((rows, LANES), jnp.float32),
            jax.ShapeDtypeStruct((rows, nb), jnp.float32),
        ],
        compiler_params=pltpu.CompilerParams(
            dimension_semantics=("arbitrary",), vmem_limit_bytes=VMEM_LIMIT),
        name="rmsnorm_in_proj",
    )(x2, g, wa, ws, wb, lng, lnb)


def _attn_kernel(kidx_ref, ka_ref, vt_ref, qit_ref, qat_ref, wt_ref, bias_ref, out_ref,
                 keys_ref, half_ref, lg_ref, p_ref, alpha_ref, m_ref, l_ref, acc_ref, *, index_bits):
    i = pl.program_id(1)
    tq, tk = ATT_TQ, ATT_TK
    n_pairs = N_HEADS_A // 2
    n_chunks = (i + 1) * (tq // tk)
    dummy_chunk = keys_ref.shape[0] - 1
    t_idx = i * tq + lax.broadcasted_iota(jnp.int32, (1, tq), 1)

    def score_chunk(c, carry):
        kc = kidx_ref[0, c]
        acc = jnp.zeros((tk, tq), jnp.float32)
        for j in range(N_IDX_HEADS // 2):
            d = jnp.dot(kc, qit_ref[0, 0, j], preferred_element_type=jnp.float32)
            acc = acc + wt_ref[0, 2 * j:2 * j + 1, :] * jnp.maximum(d[:, :tq], 0.0)
            acc = acc + wt_ref[0, 2 * j + 1:2 * j + 2, :] * jnp.maximum(d[:, tq:], 0.0)
        bits = lax.bitcast_convert_type(acc + 0.0, jnp.int32)
        key = jnp.where(bits < 0, bits ^ jnp.int32(0x7FFFFFFF), bits)
        s_idx = c * tk + lax.broadcasted_iota(jnp.int32, (tk, tq), 0)
        key = jnp.where(s_idx <= t_idx, key, jnp.int32(INT_MIN))
        keys_ref[c] = key
        half_ref[c] = lax.shift_right_arithmetic(key, 16).astype(jnp.int16)
        return carry

    lax.fori_loop(0, n_chunks, score_chunk, 0)

    k_row = jnp.minimum(t_idx + 1, TOPK_MAX)
    I16_MIN = -(2 ** 15)

    def count(pred):
        def body(c, acc):
            return acc + jnp.where(pred(c, keys_ref[c]), 1, 0).astype(jnp.int32)
        acc = lax.fori_loop(0, n_chunks, body, jnp.zeros((tk, tq), jnp.int32))
        return jnp.sum(acc, axis=0, keepdims=True)

    one16 = jnp.ones((tk, tq), jnp.int16)
    zero16 = jnp.zeros((tk, tq), jnp.int16)

    def count16(pred):
        def body(c2, acc):
            acc = acc + jnp.where(pred(half_ref[2 * c2]), one16, zero16)
            return acc + jnp.where(pred(half_ref[2 * c2 + 1]), one16, zero16)
        acc = lax.fori_loop(0, n_chunks // 2, body, jnp.zeros((tk, tq), jnp.int16))
        return jnp.sum(acc.astype(jnp.int32), axis=0, keepdims=True)

    def kth_largest16(k_need):
        def bit_round(r, prefix):
            cand = prefix + lax.shift_left(jnp.int32(1), 15 - r)
            cand16 = jnp.broadcast_to(cand, (tk, tq)).astype(jnp.int16)
            return jnp.where(count16(lambda v: v >= cand16) >= k_need, cand, prefix)
        return lax.fori_loop(0, 16, bit_round, jnp.full((1, tq), I16_MIN, jnp.int32))

    v_hi = kth_largest16(k_row)
    v_hi16 = jnp.broadcast_to(v_hi, (tk, tq)).astype(jnp.int16)
    k_low = k_row - count16(lambda v: v > v_hi16)

    def low_halves(c, carry):
        key = keys_ref[c]
        low = (key & jnp.int32(0xFFFF)) + I16_MIN
        in_bucket = lax.shift_right_arithmetic(key, 16) == v_hi
        half_ref[c] = jnp.where(in_bucket, low, I16_MIN).astype(jnp.int16)
        return carry
    lax.fori_loop(0, n_chunks, low_halves, 0)

    v_lo = kth_largest16(k_low)
    vstar = lax.shift_left(v_hi, 16) + (v_lo - I16_MIN)

    n_ge = count(lambda c, k: k >= vstar)

    @pl.when(jnp.max(n_ge - k_row) > 0)
    def _():
        need = k_row - count(lambda c, k: k > vstar)

        def idx_round(r, x):
            cand = x + lax.shift_left(jnp.int32(1), index_bits - 1 - r)

            def tie_below(c, k):
                s_idx = c * tk + lax.broadcasted_iota(jnp.int32, (tk, tq), 0)
                return (k == vstar) & (s_idx < cand)
            return jnp.where(count(tie_below) < need, cand, x)

        last = lax.fori_loop(0, index_bits, idx_round, jnp.zeros((1, tq), jnp.int32))

        def drop(c, carry):
            k = keys_ref[c]
            s_idx = c * tk + lax.broadcasted_iota(jnp.int32, (tk, tq), 0)
            keys_ref[c] = jnp.where((k == vstar) & (s_idx > last), vstar - 1, k)
            return carry
        lax.fori_loop(0, n_chunks, drop, 0)

    def to_mask(c, carry):
        maskb = jnp.where(keys_ref[c] >= vstar, 0.0, NEG_BIG)
        keys_ref[c] = lax.bitcast_convert_type(maskb, jnp.int32)
        return carry
    lax.fori_loop(0, n_chunks, to_mask, 0)
    keys_ref[dummy_chunk] = lax.bitcast_convert_type(jnp.full((tk, tq), NEG_BIG, jnp.float32), jnp.int32)

    m_ref[...] = jnp.full(m_ref.shape, NEG_BIG, jnp.float32)
    l_ref[...] = jnp.zeros(l_ref.shape, jnp.float32)
    acc_ref[...] = jnp.zeros(acc_ref.shape, jnp.float32)
    lg_ref[...] = jnp.zeros(lg_ref.shape, jnp.float32)
    p_ref[...] = jnp.zeros(p_ref.shape, p_ref.dtype)
    alpha_ref[...] = jnp.ones(alpha_ref.shape, jnp.float32)
    first_near = (tq // tk) * i - 1

    def pv_stage(c, buf):
        for h in range(N_HEADS_A):
            r0 = HEAD_DIM_A * h
            pv = jnp.dot(vt_ref[0, c, r0:r0 + HEAD_DIM_A, :], p_ref[buf, h],
                         preferred_element_type=jnp.float32)
            acc_ref[r0:r0 + HEAD_DIM_A, :] = alpha_ref[buf, h:h + 1, :] * acc_ref[r0:r0 + HEAD_DIM_A, :] + pv

    def qk_stage(c, buf):
        for j in range(n_pairs):
            d = jnp.dot(ka_ref[0, c, :, LANES * j:LANES * (j + 1)], qat_ref[0, 0, j],
                        preferred_element_type=jnp.float32)
            lg_ref[buf, 2 * j] = d[:, :tq]
            lg_ref[buf, 2 * j + 1] = d[:, tq:]

    def softmax_stage(c, valid, buf):
        cm = jnp.where(valid, c, dummy_chunk)
        tbl = jnp.where(valid, jnp.clip(c - first_near + 1, 0, 3), 0)
        maskb = lax.bitcast_convert_type(keys_ref[cm], jnp.float32)
        for h in range(N_HEADS_A):
            lm = lg_ref[buf, h] + bias_ref[tbl, h] + maskb
            m_old = m_ref[h:h + 1, :]
            m_new = jnp.maximum(m_old, jnp.max(lm, axis=0, keepdims=True))
            alpha = jnp.exp(m_old - m_new)
            p = jnp.exp(lm - m_new)
            l_ref[h:h + 1, :] = alpha * l_ref[h:h + 1, :] + jnp.sum(p, axis=0, keepdims=True)
            m_ref[h:h + 1, :] = m_new
            alpha_ref[buf, h:h + 1, :] = alpha
            p_ref[buf, h] = p.astype(p_ref.dtype)

    def step(s, buf):
        pv_stage(jnp.maximum(s - 2, 0), buf)
        qk_stage(jnp.minimum(s, n_chunks - 1), buf)
        sm = s - 1
        softmax_stage(jnp.clip(sm, 0, n_chunks - 1), (sm >= 0) & (sm < n_chunks), 1 - buf)

    def two_steps(t, carry):
        step(2 * t, 0)
        step(2 * t + 1, 1)
        return carry

    lax.fori_loop(0, (n_chunks + 2) // 2, two_steps, 0)

    for h in range(N_HEADS_A):
        r0 = HEAD_DIM_A * h
        inv = 1.0 / l_ref[h:h + 1, :]
        out_ref[0, r0:r0 + HEAD_DIM_A, :] = (acc_ref[r0:r0 + HEAD_DIM_A, :] * inv).astype(out_ref.dtype)


def _bias_tiles(rel_bias):
    tq, tk = ATT_TQ, ATT_TK
    lo = -(tq - 1)
    dist = np.arange(lo, tk + tq)
    n = np.maximum(dist, 0)
    max_exact = N_BUCKETS // 2
    nf = np.maximum(n, max_exact).astype(np.float64)
    large = max_exact + (np.log(nf / max_exact) / math.log(MAX_DISTANCE / max_exact)
                         * (N_BUCKETS - max_exact)).astype(np.int32)
    bucket = np.where(n < max_exact, n, np.minimum(large, N_BUCKETS - 1)).astype(np.int32)
    vec = (rel_bias[bucket] - rel_bias[N_BUCKETS - 1][None, :]).T
    tiles = [jnp.zeros((N_HEADS_A, tk, tq), rel_bias.dtype)]
    for base in (tk, 0, -tk):
        rows = [lax.slice_in_dim(vec, base - r - lo, base - r - lo + tq, axis=1) for r in range(tk)]
        tiles.append(jnp.stack(rows, axis=1))
    return jnp.stack(tiles)


def _sparse_attention(kidx4, ka4, vt4, qit2, qat2, wt, bias_tiles):
    b, n_c = kidx4.shape[0], kidx4.shape[1]
    seq = n_c * ATT_TK
    tq = ATT_TQ
    n_pairs = N_HEADS_A // 2
    kern = lambda *refs: _attn_kernel(*refs, index_bits=max(1, (seq - 1).bit_length()))
    return pl.pallas_call(
        kern,
        grid=(b, seq // tq),
        in_specs=[
            _resident((1, n_c, ATT_TK, LANES), lambda bi, i: (bi, 0, 0, 0)),
            _resident((1, n_c, ATT_TK, D_A), lambda bi, i: (bi, 0, 0, 0)),
            _resident((1, n_c, D_A, ATT_TK), lambda bi, i: (bi, 0, 0, 0)),
            pl.BlockSpec((1, 1, N_IDX_HEADS // 2, LANES, 2 * tq), lambda bi, i: (bi, i, 0, 0, 0)),
            pl.BlockSpec((1, 1, n_pairs, LANES, 2 * tq), lambda bi, i: (bi, i, 0, 0, 0)),
            pl.BlockSpec((1, N_IDX_HEADS, tq), lambda bi, i: (bi, 0, i)),
            _resident((4, N_HEADS_A, ATT_TK, tq), lambda bi, i: (0, 0, 0, 0)),
        ],
        out_specs=pl.BlockSpec((1, D_A, tq), lambda bi, i: (bi, 0, i)),
        out_shape=jax.ShapeDtypeStruct((b, D_A, seq), jnp.bfloat16),
        scratch_shapes=[
            pltpu.VMEM((n_c + 1, ATT_TK, tq), jnp.int32),
            pltpu.VMEM((n_c, ATT_TK, tq), jnp.int16),
            pltpu.VMEM((2, N_HEADS_A, ATT_TK, tq), jnp.float32),
            pltpu.VMEM((2, N_HEADS_A, ATT_TK, tq), jnp.bfloat16),
            pltpu.VMEM((2, N_HEADS_A, tq), jnp.float32),
            pltpu.VMEM((N_HEADS_A, tq), jnp.float32),
            pltpu.VMEM((N_HEADS_A, tq), jnp.float32),
            pltpu.VMEM((D_A, tq), jnp.float32),
        ],
        compiler_params=pltpu.CompilerParams(
            dimension_semantics=("arbitrary", "arbitrary"), vmem_limit_bytes=VMEM_LIMIT),
        name="indexer_topk_attention",
    )(kidx4, ka4, vt4, qit2, qat2, wt, bias_tiles)


def _pair_lanes(xt, seq):
    b, nh = xt.shape[0], xt.shape[1]
    tq = ATT_TQ
    x = xt.reshape(b, nh // 2, 2, LANES, seq // tq, tq)
    return jnp.transpose(x, (0, 4, 1, 3, 2, 5)).reshape(b, seq // tq, nh // 2, LANES, 2 * tq)


def _ret_kernel(q_ref, k_ref, v_ref, g_ref, cos_ref, sin_ref, dec_ref, qdec_ref, kdec_ref,
                cdec_ref, gn_ref, out_ref, state_ref):
    @pl.when(pl.program_id(1) == 0)
    def _():
        state_ref[...] = jnp.zeros(state_ref.shape, jnp.float32)

    cos = cos_ref[...]
    sin = sin_ref[...]
    half = HEAD_DIM_B // 2
    for h in range(N_HEADS_B):
        lo = HEAD_DIM_B * h
        q = q_ref[0, :, lo:lo + HEAD_DIM_B]
        k = k_ref[0, :, lo:lo + HEAD_DIM_B]
        qr = q * cos + pltpu.roll(q, half, 1) * sin
        ks = (k * cos + pltpu.roll(k, half, 1) * sin) * (HEAD_DIM_B ** -0.5)
        qb = qr.astype(jnp.bfloat16)
        vb = v_ref[0, :, lo:lo + HEAD_DIM_B].astype(jnp.bfloat16)
        qk = lax.dot_general(qb, ks.astype(jnp.bfloat16), (((1,), (1,)), ((), ())),
                             preferred_element_type=jnp.float32)
        inner = (qk * dec_ref[h]).astype(jnp.bfloat16)
        state = state_ref[h]
        y = (jnp.dot(inner, vb, preferred_element_type=jnp.float32)
             + jnp.dot(qb, state.astype(jnp.bfloat16), preferred_element_type=jnp.float32) * qdec_ref[h])
        kd_t = (ks * kdec_ref[h]).T.astype(jnp.bfloat16)
        state_ref[h] = state * cdec_ref[h] + jnp.dot(kd_t, vb, preferred_element_type=jnp.float32)
        yn = y * lax.rsqrt(jnp.mean(y * y, axis=-1, keepdims=True) + EPS) * gn_ref[:, lo:lo + HEAD_DIM_B]
        g = g_ref[0, :, lo:lo + HEAD_DIM_B]
        out_ref[0, :, lo:lo + HEAD_DIM_B] = (yn * (g * jax.nn.sigmoid(g))).astype(out_ref.dtype)


def _retention_tables(seq):
    c = RET_CHUNK
    half = HEAD_DIM_B // 2
    theta = 1.0 / (RET_THETA_BASE ** jnp.linspace(0.0, 1.0, half, dtype=jnp.float32))
    ang = jnp.arange(seq, dtype=jnp.int32).astype(jnp.float32)[:, None] * theta[None, :]
    cos = jnp.concatenate([jnp.cos(ang), jnp.cos(ang)], axis=-1)
    sin = jnp.concatenate([-jnp.sin(ang), jnp.sin(ang)], axis=-1)
    gamma = 1.0 - 2.0 ** (-5.0 - jnp.arange(N_HEADS_B, dtype=jnp.float32))
    log_g = jnp.log(gamma)
    pos = jnp.arange(c, dtype=jnp.float32)
    diff = pos[:, None] - pos[None, :]
    dec = jnp.where(diff >= 0, jnp.exp(log_g[:, None, None] * jnp.maximum(diff, 0.0)), 0.0)
    qdec = jnp.exp(log_g[:, None] * (pos[None, :] + 1.0))
    kdec = jnp.exp(log_g[:, None] * (c - 1.0 - pos[None, :]))
    cdec = jnp.exp(log_g * c)
    bc = lambda a: jnp.broadcast_to(a[:, :, None], (N_HEADS_B, c, HEAD_DIM_B))
    cdec_b = jnp.broadcast_to(cdec[:, None, None], (N_HEADS_B, 1, HEAD_DIM_B))
    return cos, sin, dec, bc(qdec), bc(kdec), cdec_b


def _retention(ob3, gn):
    b, seq, _ = ob3.shape
    c = RET_CHUNK
    cos, sin, dec, qdec, kdec, cdec = _retention_tables(seq)
    col = lambda j: (lambda bi, ci: (bi, ci, j))
    const3 = lambda bi, ci: (0, 0, 0)
    return pl.pallas_call(
        _ret_kernel,
        grid=(b, seq // c),
        in_specs=[
            pl.BlockSpec((1, c, D_B), col(0)),
            pl.BlockSpec((1, c, D_B), col(1)),
            pl.BlockSpec((1, c, D_B), col(2)),
            pl.BlockSpec((1, c, D_B), col(3)),
            pl.BlockSpec((c, HEAD_DIM_B), lambda bi, ci: (ci, 0)),
            pl.BlockSpec((c, HEAD_DIM_B), lambda bi, ci: (ci, 0)),
            _resident((N_HEADS_B, c, c), const3),
            _resident((N_HEADS_B, c, HEAD_DIM_B), const3),
            _resident((N_HEADS_B, c, HEAD_DIM_B), const3),
            _resident((N_HEADS_B, 1, HEAD_DIM_B), const3),
            _resident((1, D_B), lambda bi, ci: (0, 0)),
        ],
        out_specs=pl.BlockSpec((1, c, D_B), lambda bi, ci: (bi, ci, 0)),
        out_shape=jax.ShapeDtypeStruct((b, seq, D_B), jnp.bfloat16),
        scratch_shapes=[pltpu.VMEM((N_HEADS_B, HEAD_DIM_B, HEAD_DIM_B), jnp.float32)],
        compiler_params=pltpu.CompilerParams(
            dimension_semantics=("arbitrary", "arbitrary"), vmem_limit_bytes=VMEM_LIMIT),
        name="retention",
    )(ob3, ob3, ob3, ob3, cos, sin, dec, qdec, kdec, cdec, gn)


def _ffn_kernel(x_ref, a_ref, b_ref, woa_ref, wob_ref, gffn_ref, wg_ref, wu_ref, wd_ref, gfin_ref,
                out_ref):
    x1 = (x_ref[...]
          + jnp.dot(a_ref[...], woa_ref[...], preferred_element_type=jnp.float32)
          + jnp.dot(b_ref[...], wob_ref[...], preferred_element_type=jnp.float32))
    h2 = (x1 * lax.rsqrt(jnp.mean(x1 * x1, axis=-1, keepdims=True) + EPS) * gffn_ref[...]).astype(jnp.bfloat16)
    gate = jnp.dot(h2, wg_ref[...], preferred_element_type=jnp.float32)
    up = jnp.dot(h2, wu_ref[...], preferred_element_type=jnp.float32)
    u = (gate * jax.nn.sigmoid(gate) * up).astype(jnp.bfloat16)
    x2 = x1 + jnp.dot(u, wd_ref[...], preferred_element_type=jnp.float32)
    out_ref[...] = x2 * lax.rsqrt(jnp.mean(x2 * x2, axis=-1, keepdims=True) + EPS) * gfin_ref[...]


def _out_ffn(x2d, a2d, b2d, woa, wob, gffn, wg, wu, wd, gfin):
    rows, d_model = x2d.shape
    d_ff = wg.shape[1]
    tm = FFN_ROWS
    const = lambda i: (0, 0)
    return pl.pallas_call(
        _ffn_kernel,
        grid=(rows // tm,),
        in_specs=[
            pl.BlockSpec((tm, d_model), lambda i: (i, 0)),
            pl.BlockSpec((tm, D_A), lambda i: (i, 0)),
            pl.BlockSpec((tm, D_B), lambda i: (i, 0)),
            _resident((D_A, d_model), const),
            _resident((D_B, d_model), const),
            _resident((1, d_model), const),
            _resident((d_model, d_ff), const),
            _resident((d_model, d_ff), const),
            _resident((d_ff, d_model), const),
            _resident((1, d_model), const),
        ],
        out_specs=pl.BlockSpec((tm, d_model), lambda i: (i, 0)),
        out_shape=jax.ShapeDtypeStruct((rows, d_model), jnp.float32),
        compiler_params=pltpu.CompilerParams(
            dimension_semantics=("arbitrary",), vmem_limit_bytes=VMEM_LIMIT),
        name="out_proj_swiglu",
    )(x2d, a2d, b2d, woa, wob, gffn, wg, wu, wd, gfin)


def kernel(x, norm_mix_g, w_in, idx_k_ln_g, idx_k_ln_b, rel_bias, ret_gn_g, w_out, norm_ffn_g,
           w_gate, w_up, w_down, norm_final_g):
    b, seq, d_model = x.shape
    assert seq % ATT_TQ == 0 and seq % RET_CHUNK == 0 and (b * seq) % PROJ_ROWS == 0
    assert norm_mix_g.shape[0] == 1, "single-layer block"
    bf = jnp.bfloat16
    rows = b * seq
    n_c = seq // ATT_TK
    o = 0
    w = w_in[0]
    qa_w = w[:, o:o + D_A] * (HEAD_DIM_A ** -0.5); o += D_A
    ka_w = w[:, o:o + D_A]; o += D_A
    va_w = w[:, o:o + D_A]; o += D_A
    qi_w = w[:, o:o + N_IDX_HEADS * IDX_DIM] * (IDX_DIM ** -0.5); o += N_IDX_HEADS * IDX_DIM
    small_w = w[:, o:o + IDX_DIM + N_IDX_HEADS]; o += IDX_DIM + N_IDX_HEADS
    b_w = w[:, o:]
    wa = jnp.concatenate([qa_w, ka_w, va_w, qi_w], axis=1).astype(bf)
    ws = jnp.pad(small_w, ((0, 0), (0, LANES - small_w.shape[1]))).astype(bf)
    wb = b_w.astype(bf)
    lng = jnp.pad(idx_k_ln_g[0], (0, LANES - IDX_DIM)).reshape(1, LANES)
    lnb = jnp.pad(idx_k_ln_b[0], (0, LANES - IDX_DIM)).reshape(1, LANES)

    x2d = x.reshape(rows, d_model)
    oa, kidx, wsm, ob = _project(x2d, norm_mix_g[0].reshape(1, d_model), wa, ws, wb, lng, lnb)

    oa3 = oa.reshape(b, seq, 4 * D_A)
    qa = oa3[:, :, 0:D_A].reshape(b, seq, N_HEADS_A, HEAD_DIM_A)
    ka4 = oa3[:, :, D_A:2 * D_A].reshape(b, n_c, ATT_TK, D_A)
    vt4 = jnp.swapaxes(oa3[:, :, 2 * D_A:3 * D_A].reshape(b, n_c, ATT_TK, D_A), 2, 3)
    qi = oa3[:, :, 3 * D_A:].reshape(b, seq, N_IDX_HEADS, IDX_DIM)
    qit = jnp.pad(jnp.transpose(qi, (0, 2, 3, 1)), ((0, 0), (0, 0), (0, LANES - IDX_DIM), (0, 0)))
    qat_half = jnp.transpose(qa, (0, 2, 3, 1))
    zeros = jnp.zeros_like(qat_half)
    odd = (jnp.arange(N_HEADS_A) % 2 == 1)[None, :, None, None]
    qat = jnp.concatenate([jnp.where(odd, zeros, qat_half), jnp.where(odd, qat_half, zeros)], axis=2)
    wt = jnp.transpose(wsm.reshape(b, seq, LANES)[:, :, IDX_DIM:IDX_DIM + N_IDX_HEADS], (0, 2, 1))
    kidx4 = kidx.reshape(b, n_c, ATT_TK, LANES)

    out_a_t = _sparse_attention(kidx4, ka4, vt4, _pair_lanes(qit, seq), _pair_lanes(qat, seq), wt,
                                _bias_tiles(rel_bias))
    out_a = jnp.swapaxes(out_a_t, 1, 2).reshape(rows, D_A)

    out_b = _retention(ob.reshape(b, seq, 4 * D_B), ret_gn_g[0].reshape(1, D_B)).reshape(rows, D_B)

    wo = w_out[0]
    out = _out_ffn(x2d, out_a, out_b, wo[:D_A].astype(bf), wo[D_A:].astype(bf),
                   norm_ffn_g[0].reshape(1, d_model), w_gate[0].astype(bf), w_up[0].astype(bf),
                   w_down[0].astype(bf), norm_final_g.reshape(1, d_model))
    return out.reshape(b, seq, d_model)
```

```python
import math

import jax
import jax.numpy as jnp
import numpy as np
from jax import lax
from jax.experimental import pallas as pl
from jax.experimental.pallas import tpu as pltpu

N_HEADS_A = 8
HEAD_DIM_A = 64
N_IDX_HEADS = 8
IDX_DIM = 64
TOPK_MAX = 256
N_HEADS_B = 4
HEAD_DIM_B = 128
RET_THETA_BASE = 10000.0
N_BUCKETS = 32
MAX_DISTANCE = 128
EPS = 1e-6

D_A = N_HEADS_A * HEAD_DIM_A
D_B = N_HEADS_B * HEAD_DIM_B

LANES = 128
MXU_COLS = 256

PROJ_ROWS = 512
ATT_TQ = MXU_COLS
ATT_TK = LANES
RET_CHUNK = 256
FFN_ROWS = 256
VMEM_LIMIT = 56 * 1024 * 1024

NEG_BIG = -1e30
INT_MIN = -(2 ** 31)
assert ATT_TQ == 2 * ATT_TK


def _resident(block_shape, index_map):
    return pl.BlockSpec(block_shape, index_map, pipeline_mode=pl.Buffered(1))


def _proj_kernel(x_ref, g_ref, wa_ref, ws_ref, wb_ref, lng_ref, lnb_ref,
                 oa_ref, okidx_ref, ow_ref, ob_ref):
    x = x_ref[...]
    h = x * lax.rsqrt(jnp.mean(x * x, axis=-1, keepdims=True) + EPS) * g_ref[...]
    hb = h.astype(jnp.bfloat16)
    oa_ref[...] = jnp.dot(hb, wa_ref[...], preferred_element_type=jnp.float32).astype(oa_ref.dtype)
    ob_ref[...] = jnp.dot(hb, wb_ref[...], preferred_element_type=jnp.float32)
    s = jnp.dot(hb, ws_ref[...], preferred_element_type=jnp.float32)
    lane = lax.broadcasted_iota(jnp.int32, s.shape, 1)
    is_k = lane < IDX_DIM
    mu = jnp.sum(jnp.where(is_k, s, 0.0), axis=-1, keepdims=True) * (1.0 / IDX_DIM)
    d = jnp.where(is_k, s - mu, 0.0)
    var = jnp.sum(d * d, axis=-1, keepdims=True) * (1.0 / IDX_DIM)
    kn = d * lax.rsqrt(var + EPS) * lng_ref[...] + lnb_ref[...]
    okidx_ref[...] = jnp.where(is_k, kn, 0.0).astype(okidx_ref.dtype)
    ow_ref[...] = s * (N_IDX_HEADS ** -0.5)


def _project(x2, g, wa, ws, wb, lng, lnb):
    rows, d_model = x2.shape
    tm = PROJ_ROWS
    na, nb = wa.shape[1], wb.shape[1]
    const = lambda i: (0, 0)
    return pl.pallas_call(
        _proj_kernel,
        grid=(rows // tm,),
        in_specs=[
            pl.BlockSpec((tm, d_model), lambda i: (i, 0)),
            _resident((1, d_model), const),
            _resident((d_model, na), const),
            _resident((d_model, LANES), const),
            _resident((d_model, nb), const),
            _resident((1, LANES), const),
            _resident((1, LANES), const),
        ],
        out_specs=[
            pl.BlockSpec((tm, na), lambda i: (i, 0)),
            pl.BlockSpec((tm, LANES), lambda i: (i, 0)),
            pl.BlockSpec((tm, LANES), lambda i: (i, 0)),
            pl.BlockSpec((tm, nb), lambda i: (i, 0)),
        ],
        out_shape=[
            jax.ShapeDtypeStruct((rows, na), jnp.bfloat16),
            jax.ShapeDtypeStruct((rows, LANES), jnp.bfloat16),
            jax.ShapeDtypeStruct((rows, LANES), jnp.float32),
            jax.ShapeDtypeStruct((rows, nb), jnp.float32),
        ],
        compiler_params=pltpu.CompilerParams(
            dimension_semantics=("arbitrary",), vmem_limit_bytes=VMEM_LIMIT),
        name="rmsnorm_in_proj",
    )(x2, g, wa, ws, wb, lng, lnb)


def _attn_kernel(kidx_ref, ka_ref, vt_ref, qit_ref, qat_ref, wt_ref, bias_ref, out_ref,
                 keys_ref, half_ref, lg_ref, p_ref, alpha_ref, m_ref, l_ref, acc_ref, *, index_bits):
    i = pl.program_id(1)
    tq, tk = ATT_TQ, ATT_TK
    n_pairs = N_HEADS_A // 2
    n_chunks = (i + 1) * (tq // tk)
    dummy_chunk = keys_ref.shape[0] - 1
    t_idx = i * tq + lax.broadcasted_iota(jnp.int32, (1, tq), 1)

    lg_ref[...] = jnp.zeros(lg_ref.shape, jnp.float32)

    def idx_dots(c, buf):
        kc = kidx_ref[0, c]
        for j in range(N_IDX_HEADS // 2):
            d = jnp.dot(kc, qit_ref[0, 0, j], preferred_element_type=jnp.float32)
            lg_ref[buf, 2 * j] = d[:, :tq]
            lg_ref[buf, 2 * j + 1] = d[:, tq:]

    def idx_reduce(c, valid, buf):
        cw = jnp.where(valid, c, dummy_chunk)
        for hf in range(tq // LANES):
            ls = slice(hf * LANES, (hf + 1) * LANES)
            acc = jnp.zeros((tk, LANES), jnp.float32)
            for h in range(N_IDX_HEADS):
                acc = acc + wt_ref[0, h:h + 1, ls] * jnp.maximum(lg_ref[buf, h, :, ls], 0.0)
            bits = lax.bitcast_convert_type(acc + 0.0, jnp.int32)
            key = jnp.where(bits < 0, bits ^ jnp.int32(0x7FFFFFFF), bits)
            s_idx = c * tk + lax.broadcasted_iota(jnp.int32, (tk, LANES), 0)
            key = jnp.where(s_idx <= t_idx[:, ls], key, jnp.int32(INT_MIN))
            keys_ref[cw, :, ls] = key
            half_ref[cw, :, ls] = lax.shift_right_arithmetic(key, 16).astype(jnp.int16)

    def idx_step(s, buf):
        idx_dots(jnp.minimum(s, n_chunks - 1), buf)
        idx_reduce(jnp.clip(s - 1, 0, n_chunks - 1), (s >= 1) & (s <= n_chunks), 1 - buf)

    def idx_two_steps(t, carry):
        idx_step(2 * t, 0)
        idx_step(2 * t + 1, 1)
        return carry

    lax.fori_loop(0, (n_chunks + 2) // 2, idx_two_steps, 0)

    k_row = jnp.minimum(t_idx + 1, TOPK_MAX)
    I16_MIN = -(2 ** 15)

    def count(pred):
        def body(c, acc):
            return acc + jnp.where(pred(c, keys_ref[c]), 1, 0).astype(jnp.int32)
        acc = lax.fori_loop(0, n_chunks, body, jnp.zeros((tk, tq), jnp.int32))
        return jnp.sum(acc, axis=0, keepdims=True)

    one16 = jnp.ones((tk, tq), jnp.int16)
    zero16 = jnp.zeros((tk, tq), jnp.int16)

    def count16(pred):
        def body(c2, acc):
            acc = acc + jnp.where(pred(half_ref[2 * c2]), one16, zero16)
            return acc + jnp.where(pred(half_ref[2 * c2 + 1]), one16, zero16)
        acc = lax.fori_loop(0, n_chunks // 2, body, jnp.zeros((tk, tq), jnp.int16))
        return jnp.sum(acc.astype(jnp.int32), axis=0, keepdims=True)

    def kth_largest16(k_need):
        def bit_round(r, prefix):
            cand = prefix + lax.shift_left(jnp.int32(1), 15 - r)
            cand16 = jnp.broadcast_to(cand, (tk, tq)).astype(jnp.int16)
            return jnp.where(count16(lambda v: v >= cand16) >= k_need, cand, prefix)
        return lax.fori_loop(0, 16, bit_round, jnp.full((1, tq), I16_MIN, jnp.int32))

    v_hi = kth_largest16(k_row)
    v_hi16 = jnp.broadcast_to(v_hi, (tk, tq)).astype(jnp.int16)
    k_low = k_row - count16(lambda v: v > v_hi16)

    def low_halves(c, carry):
        key = keys_ref[c]
        low = (key & jnp.int32(0xFFFF)) + I16_MIN
        in_bucket = lax.shift_right_arithmetic(key, 16) == v_hi
        half_ref[c] = jnp.where(in_bucket, low, I16_MIN).astype(jnp.int16)
        return carry
    lax.fori_loop(0, n_chunks, low_halves, 0)

    v_lo = kth_largest16(k_low)
    vstar = lax.shift_left(v_hi, 16) + (v_lo - I16_MIN)

    n_ge = count(lambda c, k: k >= vstar)

    @pl.when(jnp.max(n_ge - k_row) > 0)
    def _():
        need = k_row - count(lambda c, k: k > vstar)

        def idx_round(r, x):
            cand = x + lax.shift_left(jnp.int32(1), index_bits - 1 - r)

            def tie_below(c, k):
                s_idx = c * tk + lax.broadcasted_iota(jnp.int32, (tk, tq), 0)
                return (k == vstar) & (s_idx < cand)
            return jnp.where(count(tie_below) < need, cand, x)

        last = lax.fori_loop(0, index_bits, idx_round, jnp.zeros((1, tq), jnp.int32))

        def drop(c, carry):
            k = keys_ref[c]
            s_idx = c * tk + lax.broadcasted_iota(jnp.int32, (tk, tq), 0)
            keys_ref[c] = jnp.where((k == vstar) & (s_idx > last), vstar - 1, k)
            return carry
        lax.fori_loop(0, n_chunks, drop, 0)

    def to_mask(c, carry):
        maskb = jnp.where(keys_ref[c] >= vstar, 0.0, NEG_BIG)
        keys_ref[c] = lax.bitcast_convert_type(maskb, jnp.int32)
        return carry
    lax.fori_loop(0, n_chunks, to_mask, 0)
    keys_ref[dummy_chunk] = lax.bitcast_convert_type(jnp.full((tk, tq), NEG_BIG, jnp.float32), jnp.int32)

    m_ref[...] = jnp.full(m_ref.shape, NEG_BIG, jnp.float32)
    l_ref[...] = jnp.zeros(l_ref.shape, jnp.float32)
    acc_ref[...] = jnp.zeros(acc_ref.shape, jnp.float32)
    p_ref[...] = jnp.zeros(p_ref.shape, p_ref.dtype)
    alpha_ref[...] = jnp.ones(alpha_ref.shape, jnp.float32)
    first_near = (tq // tk) * i - 1

    def pv_stage(c, buf):
        for h in range(N_HEADS_A):
            r0 = HEAD_DIM_A * h
            pv = jnp.dot(vt_ref[0, c, r0:r0 + HEAD_DIM_A, :], p_ref[buf, h],
                         preferred_element_type=jnp.float32)
            acc_ref[r0:r0 + HEAD_DIM_A, :] = alpha_ref[buf, h:h + 1, :] * acc_ref[r0:r0 + HEAD_DIM_A, :] + pv

    def qk_stage(c, buf):
        for j in range(n_pairs):
            d = jnp.dot(ka_ref[0, c, :, LANES * j:LANES * (j + 1)], qat_ref[0, 0, j],
                        preferred_element_type=jnp.float32)
            lg_ref[buf, 2 * j] = d[:, :tq]
            lg_ref[buf, 2 * j + 1] = d[:, tq:]

    def softmax_stage(c, valid, buf, near):
        cm = jnp.where(valid, c, dummy_chunk)
        tbl = jnp.where(valid, jnp.clip(c - first_near + 1, 0, 3), 0)
        for h in range(N_HEADS_A):
            for hf in range(tq // LANES):
                ls = slice(hf * LANES, (hf + 1) * LANES)
                lm = lg_ref[buf, h, :, ls] + lax.bitcast_convert_type(keys_ref[cm, :, ls], jnp.float32)
                if near:
                    lm = lm + bias_ref[tbl, h, :, ls]
                m_old = m_ref[h:h + 1, ls]
                m_new = jnp.maximum(m_old, jnp.max(lm, axis=0, keepdims=True))
                alpha = jnp.exp(m_old - m_new)
                p = jnp.exp(lm - m_new)
                l_ref[h:h + 1, ls] = alpha * l_ref[h:h + 1, ls] + jnp.sum(p, axis=0, keepdims=True)
                m_ref[h:h + 1, ls] = m_new
                alpha_ref[buf, h:h + 1, ls] = alpha
                p_ref[buf, h, :, ls] = p.astype(p_ref.dtype)

    def step(s, buf, near):
        qk_stage(jnp.minimum(s, n_chunks - 1), buf)
        pv_stage(jnp.maximum(s - 2, 0), buf)
        sm = s - 1
        softmax_stage(jnp.clip(sm, 0, n_chunks - 1), (sm >= 0) & (sm < n_chunks), 1 - buf, near)

    def two_steps(near):
        def body(t, carry):
            step(2 * t, 0, near)
            step(2 * t + 1, 1, near)
            return carry
        return body

    n_pairs_of_steps = (n_chunks + 2) // 2
    lax.fori_loop(0, n_pairs_of_steps - 2, two_steps(False), 0)
    lax.fori_loop(n_pairs_of_steps - 2, n_pairs_of_steps, two_steps(True), 0)

    for h in range(N_HEADS_A):
        r0 = HEAD_DIM_A * h
        inv = 1.0 / l_ref[h:h + 1, :]
        out_ref[0, r0:r0 + HEAD_DIM_A, :] = (acc_ref[r0:r0 + HEAD_DIM_A, :] * inv).astype(out_ref.dtype)


def _bias_tiles(rel_bias):
    tq, tk = ATT_TQ, ATT_TK
    lo = -(tq - 1)
    dist = np.arange(lo, tk + tq)
    n = np.maximum(dist, 0)
    max_exact = N_BUCKETS // 2
    nf = np.maximum(n, max_exact).astype(np.float64)
    large = max_exact + (np.log(nf / max_exact) / math.log(MAX_DISTANCE / max_exact)
                         * (N_BUCKETS - max_exact)).astype(np.int32)
    bucket = np.where(n < max_exact, n, np.minimum(large, N_BUCKETS - 1)).astype(np.int32)
    vec = (rel_bias[bucket] - rel_bias[N_BUCKETS - 1][None, :]).T
    tiles = [jnp.zeros((N_HEADS_A, tk, tq), rel_bias.dtype)]
    for base in (tk, 0, -tk):
        rows = [lax.slice_in_dim(vec, base - r - lo, base - r - lo + tq, axis=1) for r in range(tk)]
        tiles.append(jnp.stack(rows, axis=1))
    return jnp.stack(tiles)


def _sparse_attention(kidx4, ka4, vt4, qit2, qat2, wt, bias_tiles):
    b, n_c = kidx4.shape[0], kidx4.shape[1]
    seq = n_c * ATT_TK
    tq = ATT_TQ
    n_pairs = N_HEADS_A // 2
    kern = lambda *refs: _attn_kernel(*refs, index_bits=max(1, (seq - 1).bit_length()))
    return pl.pallas_call(
        kern,
        grid=(b, seq // tq),
        in_specs=[
            _resident((1, n_c, ATT_TK, LANES), lambda bi, i: (bi, 0, 0, 0)),
            _resident((1, n_c, ATT_TK, D_A), lambda bi, i: (bi, 0, 0, 0)),
            _resident((1, n_c, D_A, ATT_TK), lambda bi, i: (bi, 0, 0, 0)),
            pl.BlockSpec((1, 1, N_IDX_HEADS // 2, LANES, 2 * tq), lambda bi, i: (bi, i, 0, 0, 0)),
            pl.BlockSpec((1, 1, n_pairs, LANES, 2 * tq), lambda bi, i: (bi, i, 0, 0, 0)),
            pl.BlockSpec((1, N_IDX_HEADS, tq), lambda bi, i: (bi, 0, i)),
            _resident((4, N_HEADS_A, ATT_TK, tq), lambda bi, i: (0, 0, 0, 0)),
        ],
        out_specs=pl.BlockSpec((1, D_A, tq), lambda bi, i: (bi, 0, i)),
        out_shape=jax.ShapeDtypeStruct((b, D_A, seq), jnp.bfloat16),
        scratch_shapes=[
            pltpu.VMEM((n_c + 1, ATT_TK, tq), jnp.int32),
            pltpu.VMEM((n_c + 1, ATT_TK, tq), jnp.int16),
            pltpu.VMEM((2, N_HEADS_A, ATT_TK, tq), jnp.float32),
            pltpu.VMEM((2, N_HEADS_A, ATT_TK, tq), jnp.bfloat16),
            pltpu.VMEM((2, N_HEADS_A, tq), jnp.float32),
            pltpu.VMEM((N_HEADS_A, tq), jnp.float32),
            pltpu.VMEM((N_HEADS_A, tq), jnp.float32),
            pltpu.VMEM((D_A, tq), jnp.float32),
        ],
        compiler_params=pltpu.CompilerParams(
            dimension_semantics=("arbitrary", "arbitrary"), vmem_limit_bytes=VMEM_LIMIT),
        name="indexer_topk_attention",
    )(kidx4, ka4, vt4, qit2, qat2, wt, bias_tiles)


def _pair_lanes(xt, seq):
    b, nh = xt.shape[0], xt.shape[1]
    tq = ATT_TQ
    x = xt.reshape(b, nh // 2, 2, LANES, seq // tq, tq)
    return jnp.transpose(x, (0, 4, 1, 3, 2, 5)).reshape(b, seq // tq, nh // 2, LANES, 2 * tq)


def _ret_kernel(q_ref, k_ref, v_ref, g_ref, cos_ref, sin_ref, dec_ref, qdec_ref, kdec_ref,
                cdec_ref, gn_ref, out_ref, state_ref):
    @pl.when(pl.program_id(1) == 0)
    def _():
        state_ref[...] = jnp.zeros(state_ref.shape, jnp.float32)

    cos = cos_ref[...]
    sin = sin_ref[...]
    half = HEAD_DIM_B // 2
    for h in range(N_HEADS_B):
        lo = HEAD_DIM_B * h
        q = q_ref[0, :, lo:lo + HEAD_DIM_B]
        k = k_ref[0, :, lo:lo + HEAD_DIM_B]
        qr = q * cos + pltpu.roll(q, half, 1) * sin
        ks = (k * cos + pltpu.roll(k, half, 1) * sin) * (HEAD_DIM_B ** -0.5)
        qb = qr.astype(jnp.bfloat16)
        vb = v_ref[0, :, lo:lo + HEAD_DIM_B].astype(jnp.bfloat16)
        qk = lax.dot_general(qb, ks.astype(jnp.bfloat16), (((1,), (1,)), ((), ())),
                             preferred_element_type=jnp.float32)
        inner = (qk * dec_ref[h]).astype(jnp.bfloat16)
        state = state_ref[h]
        y = (jnp.dot(inner, vb, preferred_element_type=jnp.float32)
             + jnp.dot(qb, state.astype(jnp.bfloat16), preferred_element_type=jnp.float32) * qdec_ref[h])
        kd_t = (ks * kdec_ref[h]).T.astype(jnp.bfloat16)
        state_ref[h] = state * cdec_ref[h] + jnp.dot(kd_t, vb, preferred_element_type=jnp.float32)
        yn = y * lax.rsqrt(jnp.mean(y * y, axis=-1, keepdims=True) + EPS) * gn_ref[:, lo:lo + HEAD_DIM_B]
        g = g_ref[0, :, lo:lo + HEAD_DIM_B]
        out_ref[0, :, lo:lo + HEAD_DIM_B] = (yn * (g * jax.nn.sigmoid(g))).astype(out_ref.dtype)


def _retention_tables(seq):
    c = RET_CHUNK
    half = HEAD_DIM_B // 2
    theta = 1.0 / (RET_THETA_BASE ** jnp.linspace(0.0, 1.0, half, dtype=jnp.float32))
    ang = jnp.arange(seq, dtype=jnp.int32).astype(jnp.float32)[:, None] * theta[None, :]
    cos = jnp.concatenate([jnp.cos(ang), jnp.cos(ang)], axis=-1)
    sin = jnp.concatenate([-jnp.sin(ang), jnp.sin(ang)], axis=-1)
    gamma = 1.0 - 2.0 ** (-5.0 - jnp.arange(N_HEADS_B, dtype=jnp.float32))
    log_g = jnp.log(gamma)
    pos = jnp.arange(c, dtype=jnp.float32)
    diff = pos[:, None] - pos[None, :]
    dec = jnp.where(diff >= 0, jnp.exp(log_g[:, None, None] * jnp.maximum(diff, 0.0)), 0.0)
    qdec = jnp.exp(log_g[:, None] * (pos[None, :] + 1.0))
    kdec = jnp.exp(log_g[:, None] * (c - 1.0 - pos[None, :]))
    cdec = jnp.exp(log_g * c)
    bc = lambda a: jnp.broadcast_to(a[:, :, None], (N_HEADS_B, c, HEAD_DIM_B))
    cdec_b = jnp.broadcast_to(cdec[:, None, None], (N_HEADS_B, 1, HEAD_DIM_B))
    return cos, sin, dec, bc(qdec), bc(kdec), cdec_b


def _retention(ob3, gn):
    b, seq, _ = ob3.shape
    c = RET_CHUNK
    cos, sin, dec, qdec, kdec, cdec = _retention_tables(seq)
    col = lambda j: (lambda bi, ci: (bi, ci, j))
    const3 = lambda bi, ci: (0, 0, 0)
    return pl.pallas_call(
        _ret_kernel,
        grid=(b, seq // c),
        in_specs=[
            pl.BlockSpec((1, c, D_B), col(0)),
            pl.BlockSpec((1, c, D_B), col(1)),
            pl.BlockSpec((1, c, D_B), col(2)),
            pl.BlockSpec((1, c, D_B), col(3)),
            pl.BlockSpec((c, HEAD_DIM_B), lambda bi, ci: (ci, 0)),
            pl.BlockSpec((c, HEAD_DIM_B), lambda bi, ci: (ci, 0)),
            _resident((N_HEADS_B, c, c), const3),
            _resident((N_HEADS_B, c, HEAD_DIM_B), const3),
            _resident((N_HEADS_B, c, HEAD_DIM_B), const3),
            _resident((N_HEADS_B, 1, HEAD_DIM_B), const3),
            _resident((1, D_B), lambda bi, ci: (0, 0)),
        ],
        out_specs=pl.BlockSpec((1, c, D_B), lambda bi, ci: (bi, ci, 0)),
        out_shape=jax.ShapeDtypeStruct((b, seq, D_B), jnp.bfloat16),
        scratch_shapes=[pltpu.VMEM((N_HEADS_B, HEAD_DIM_B, HEAD_DIM_B), jnp.float32)],
        compiler_params=pltpu.CompilerParams(
            dimension_semantics=("arbitrary", "arbitrary"), vmem_limit_bytes=VMEM_LIMIT),
        name="retention",
    )(ob3, ob3, ob3, ob3, cos, sin, dec, qdec, kdec, cdec, gn)


def _ffn_kernel(x_ref, a_ref, b_ref, woa_ref, wob_ref, gffn_ref, wg_ref, wu_ref, wd_ref, gfin_ref,
                out_ref):
    x1 = (x_ref[...]
          + jnp.dot(a_ref[...], woa_ref[...], preferred_element_type=jnp.float32)
          + jnp.dot(b_ref[...], wob_ref[...], preferred_element_type=jnp.float32))
    h2 = (x1 * lax.rsqrt(jnp.mean(x1 * x1, axis=-1, keepdims=True) + EPS) * gffn_ref[...]).astype(jnp.bfloat16)
    gate = jnp.dot(h2, wg_ref[...], preferred_element_type=jnp.float32)
    up = jnp.dot(h2, wu_ref[...], preferred_element_type=jnp.float32)
    u = (gate * jax.nn.sigmoid(gate) * up).astype(jnp.bfloat16)
    x2 = x1 + jnp.dot(u, wd_ref[...], preferred_element_type=jnp.float32)
    out_ref[...] = x2 * lax.rsqrt(jnp.mean(x2 * x2, axis=-1, keepdims=True) + EPS) * gfin_ref[...]


def _out_ffn(x2d, a2d, b2d, woa, wob, gffn, wg, wu, wd, gfin):
    rows, d_model = x2d.shape
    d_ff = wg.shape[1]
    tm = FFN_ROWS
    const = lambda i: (0, 0)
    return pl.pallas_call(
        _ffn_kernel,
        grid=(rows // tm,),
        in_specs=[
            pl.BlockSpec((tm, d_model), lambda i: (i, 0)),
            pl.BlockSpec((tm, D_A), lambda i: (i, 0)),
            pl.BlockSpec((tm, D_B), lambda i: (i, 0)),
            _resident((D_A, d_model), const),
            _resident((D_B, d_model), const),
            _resident((1, d_model), const),
            _resident((d_model, d_ff), const),
            _resident((d_model, d_ff), const),
            _resident((d_ff, d_model), const),
            _resident((1, d_model), const),
        ],
        out_specs=pl.BlockSpec((tm, d_model), lambda i: (i, 0)),
        out_shape=jax.ShapeDtypeStruct((rows, d_model), jnp.float32),
        compiler_params=pltpu.CompilerParams(
            dimension_semantics=("arbitrary",), vmem_limit_bytes=VMEM_LIMIT),
        name="out_proj_swiglu",
    )(x2d, a2d, b2d, woa, wob, gffn, wg, wu, wd, gfin)


def kernel(x, norm_mix_g, w_in, idx_k_ln_g, idx_k_ln_b, rel_bias, ret_gn_g, w_out, norm_ffn_g,
           w_gate, w_up, w_down, norm_final_g):
    b, seq, d_model = x.shape
    assert seq % ATT_TQ == 0 and seq % RET_CHUNK == 0 and (b * seq) % PROJ_ROWS == 0
    assert norm_mix_g.shape[0] == 1, "single-layer block"
    bf = jnp.bfloat16
    rows = b * seq
    n_c = seq // ATT_TK
    o = 0
    w = w_in[0]
    qa_w = w[:, o:o + D_A] * (HEAD_DIM_A ** -0.5); o += D_A
    ka_w = w[:, o:o + D_A]; o += D_A
    va_w = w[:, o:o + D_A]; o += D_A
    qi_w = w[:, o:o + N_IDX_HEADS * IDX_DIM] * (IDX_DIM ** -0.5); o += N_IDX_HEADS * IDX_DIM
    small_w = w[:, o:o + IDX_DIM + N_IDX_HEADS]; o += IDX_DIM + N_IDX_HEADS
    b_w = w[:, o:]
    wa = jnp.concatenate([qa_w, ka_w, va_w, qi_w], axis=1).astype(bf)
    ws = jnp.pad(small_w, ((0, 0), (0, LANES - small_w.shape[1]))).astype(bf)
    wb = b_w.astype(bf)
    lng = jnp.pad(idx_k_ln_g[0], (0, LANES - IDX_DIM)).reshape(1, LANES)
    lnb = jnp.pad(idx_k_ln_b[0], (0, LANES - IDX_DIM)).reshape(1, LANES)

    x2d = x.reshape(rows, d_model)
    oa, kidx, wsm, ob = _project(x2d, norm_mix_g[0].reshape(1, d_model), wa, ws, wb, lng, lnb)

    oa3 = oa.reshape(b, seq, 4 * D_A)
    qa = oa3[:, :, 0:D_A].reshape(b, seq, N_HEADS_A, HEAD_DIM_A)
    ka4 = oa3[:, :, D_A:2 * D_A].reshape(b, n_c, ATT_TK, D_A)
    vt4 = jnp.swapaxes(oa3[:, :, 2 * D_A:3 * D_A].reshape(b, n_c, ATT_TK, D_A), 2, 3)
    qi = oa3[:, :, 3 * D_A:].reshape(b, seq, N_IDX_HEADS, IDX_DIM)
    qit = jnp.pad(jnp.transpose(qi, (0, 2, 3, 1)), ((0, 0), (0, 0), (0, LANES - IDX_DIM), (0, 0)))
    qat_half = jnp.transpose(qa, (0, 2, 3, 1))
    zeros = jnp.zeros_like(qat_half)
    odd = (jnp.arange(N_HEADS_A) % 2 == 1)[None, :, None, None]
    qat = jnp.concatenate([jnp.where(odd, zeros, qat_half), jnp.where(odd, qat_half, zeros)], axis=2)
    wt = jnp.transpose(wsm.reshape(b, seq, LANES)[:, :, IDX_DIM:IDX_DIM + N_IDX_HEADS], (0, 2, 1))
    kidx4 = kidx.reshape(b, n_c, ATT_TK, LANES)

    out_a_t = _sparse_attention(kidx4, ka4, vt4, _pair_lanes(qit, seq), _pair_lanes(qat, seq), wt,
                                _bias_tiles(rel_bias))
    out_a = jnp.swapaxes(out_a_t, 1, 2).reshape(rows, D_A)

    out_b = _retention(ob.reshape(b, seq, 4 * D_B), ret_gn_g[0].reshape(1, D_B)).reshape(rows, D_B)

    wo = w_out[0]
    out = _out_ffn(x2d, out_a, out_b, wo[:D_A].astype(bf), wo[D_A:].astype(bf),
                   norm_ffn_g[0].reshape(1, d_model), w_gate[0].astype(bf), w_up[0].astype(bf),
                   w_down[0].astype(bf), norm_final_g.reshape(1, d_model))
    return out.reshape(b, seq, d_model)
```

```python
import math

import jax
import jax.numpy as jnp
import numpy as np
from jax import lax
from jax.experimental import pallas as pl
from jax.experimental.pallas import tpu as pltpu

N_HEADS_A = 8
HEAD_DIM_A = 64
N_IDX_HEADS = 8
IDX_DIM = 64
TOPK_MAX = 256
N_HEADS_B = 4
HEAD_DIM_B = 128
RET_THETA_BASE = 10000.0
N_BUCKETS = 32
MAX_DISTANCE = 128
EPS = 1e-6

D_A = N_HEADS_A * HEAD_DIM_A
D_B = N_HEADS_B * HEAD_DIM_B

LANES = 128
MXU_COLS = 256

PROJ_ROWS = 512
ATT_TQ = MXU_COLS
ATT_TK = LANES
RET_CHUNK = 256
FFN_ROWS = 256
VMEM_LIMIT = 56 * 1024 * 1024

NEG_BIG = -1e30
INT_MIN = -(2 ** 31)
assert ATT_TQ == 2 * ATT_TK


def _resident(block_shape, index_map):
    return pl.BlockSpec(block_shape, index_map, pipeline_mode=pl.Buffered(1))


def _proj_kernel(x_ref, g_ref, wa_ref, ws_ref, wb_ref, lng_ref, lnb_ref,
                 qat_ref, qit_ref, ka_ref, vt_ref, okidx_ref, owt_ref, ob_ref):
    bf = jnp.bfloat16
    x = x_ref[...]
    tm = x.shape[0]
    h = x * lax.rsqrt(jnp.mean(x * x, axis=-1, keepdims=True) + EPS) * g_ref[...]
    hb = h.astype(bf)

    def part(k):
        return jnp.dot(hb, wa_ref[:, k * D_A:(k + 1) * D_A], preferred_element_type=jnp.float32)

    ka_ref[...] = part(1).astype(bf)
    va = part(2)
    for r in range(tm // ATT_TK):
        vt_ref[r] = va[r * ATT_TK:(r + 1) * ATT_TK, :].T.astype(bf)
    qa = part(0)
    qi = part(3)
    hd, tq = HEAD_DIM_A, ATT_TQ
    zero_q = jnp.zeros((hd, tq), bf)
    for qb in range(tm // tq):
        qat = qa[qb * tq:(qb + 1) * tq, :].T.astype(bf)
        qit = qi[qb * tq:(qb + 1) * tq, :].T.astype(bf)
        for j in range(N_HEADS_A // 2):
            even = slice(2 * j * hd, (2 * j + 1) * hd)
            odd = slice((2 * j + 1) * hd, (2 * j + 2) * hd)
            qat_ref[qb, j, 0:hd, 0:tq] = qat[even]
            qat_ref[qb, j, hd:2 * hd, 0:tq] = zero_q
            qat_ref[qb, j, 0:hd, tq:2 * tq] = zero_q
            qat_ref[qb, j, hd:2 * hd, tq:2 * tq] = qat[odd]
            qit_ref[qb, j, 0:hd, 0:tq] = qit[even]
            qit_ref[qb, j, 0:hd, tq:2 * tq] = qit[odd]
            qit_ref[qb, j, hd:2 * hd, 0:tq] = zero_q
            qit_ref[qb, j, hd:2 * hd, tq:2 * tq] = zero_q
    ob_ref[...] = jnp.dot(hb, wb_ref[...], preferred_element_type=jnp.float32)
    s = jnp.dot(hb, ws_ref[...], preferred_element_type=jnp.float32)
    lane = lax.broadcasted_iota(jnp.int32, s.shape, 1)
    is_k = lane < IDX_DIM
    mu = jnp.sum(jnp.where(is_k, s, 0.0), axis=-1, keepdims=True) * (1.0 / IDX_DIM)
    d = jnp.where(is_k, s - mu, 0.0)
    var = jnp.sum(d * d, axis=-1, keepdims=True) * (1.0 / IDX_DIM)
    kn = d * lax.rsqrt(var + EPS) * lng_ref[...] + lnb_ref[...]
    okidx_ref[...] = jnp.where(is_k, kn, 0.0).astype(okidx_ref.dtype)
    owt_ref[...] = (s * (N_IDX_HEADS ** -0.5)).T[IDX_DIM:IDX_DIM + N_IDX_HEADS, :]


def _project(x2, g, wa, ws, wb, lng, lnb):
    rows, d_model = x2.shape
    tm = PROJ_ROWS
    na, nb = wa.shape[1], wb.shape[1]
    n_pairs = N_HEADS_A // 2
    const = lambda i: (0, 0)
    return pl.pallas_call(
        _proj_kernel,
        grid=(rows // tm,),
        in_specs=[
            pl.BlockSpec((tm, d_model), lambda i: (i, 0)),
            _resident((1, d_model), const),
            _resident((d_model, na), const),
            _resident((d_model, LANES), const),
            _resident((d_model, nb), const),
            _resident((1, LANES), const),
            _resident((1, LANES), const),
        ],
        out_specs=[
            pl.BlockSpec((tm // ATT_TQ, n_pairs, LANES, 2 * ATT_TQ), lambda i: (i, 0, 0, 0)),
            pl.BlockSpec((tm // ATT_TQ, n_pairs, LANES, 2 * ATT_TQ), lambda i: (i, 0, 0, 0)),
            pl.BlockSpec((tm, D_A), lambda i: (i, 0)),
            pl.BlockSpec((tm // ATT_TK, D_A, ATT_TK), lambda i: (i, 0, 0)),
            pl.BlockSpec((tm, LANES), lambda i: (i, 0)),
            pl.BlockSpec((N_IDX_HEADS, tm), lambda i: (0, i)),
            pl.BlockSpec((tm, nb), lambda i: (i, 0)),
        ],
        out_shape=[
            jax.ShapeDtypeStruct((rows // ATT_TQ, n_pairs, LANES, 2 * ATT_TQ), jnp.bfloat16),
            jax.ShapeDtypeStruct((rows // ATT_TQ, n_pairs, LANES, 2 * ATT_TQ), jnp.bfloat16),
            jax.ShapeDtypeStruct((rows, D_A), jnp.bfloat16),
            jax.ShapeDtypeStruct((rows // ATT_TK, D_A, ATT_TK), jnp.bfloat16),
            jax.ShapeDtypeStruct((rows, LANES), jnp.bfloat16),
            jax.ShapeDtypeStruct((N_IDX_HEADS, rows), jnp.float32),
            jax.ShapeDtypeStruct((rows, nb), jnp.float32),
        ],
        compiler_params=pltpu.CompilerParams(
            dimension_semantics=("arbitrary",), vmem_limit_bytes=VMEM_LIMIT),
        name="rmsnorm_in_proj",
    )(x2, g, wa, ws, wb, lng, lnb)


def _attn_kernel(kidx_ref, ka_ref, vt_ref, qit_ref, qat_ref, wt_ref, bias_ref, out_ref,
                 keys_ref, half_ref, lg_ref, p_ref, alpha_ref, m_ref, l_ref, acc_ref, *, index_bits):
    i = pl.program_id(1)
    tq, tk = ATT_TQ, ATT_TK
    n_pairs = N_HEADS_A // 2
    n_chunks = (i + 1) * (tq // tk)
    dummy_chunk = keys_ref.shape[0] - 1
    t_idx = i * tq + lax.broadcasted_iota(jnp.int32, (1, tq), 1)

    lg_ref[...] = jnp.zeros(lg_ref.shape, jnp.float32)

    def idx_dots(c, buf):
        kc = kidx_ref[0, c]
        for j in range(N_IDX_HEADS // 2):
            d = jnp.dot(kc, qit_ref[0, 0, j], preferred_element_type=jnp.float32)
            lg_ref[buf, 2 * j] = d[:, :tq]
            lg_ref[buf, 2 * j + 1] = d[:, tq:]

    def idx_reduce(c, valid, buf):
        cw = jnp.where(valid, c, dummy_chunk)
        for hf in range(tq // LANES):
            ls = slice(hf * LANES, (hf + 1) * LANES)
            acc = jnp.zeros((tk, LANES), jnp.float32)
            for h in range(N_IDX_HEADS):
                acc = acc + wt_ref[h:h + 1, ls] * jnp.maximum(lg_ref[buf, h, :, ls], 0.0)
            bits = lax.bitcast_convert_type(acc + 0.0, jnp.int32)
            key = jnp.where(bits < 0, bits ^ jnp.int32(0x7FFFFFFF), bits)
            s_idx = c * tk + lax.broadcasted_iota(jnp.int32, (tk, LANES), 0)
            key = jnp.where(s_idx <= t_idx[:, ls], key, jnp.int32(INT_MIN))
            keys_ref[cw, :, ls] = key
            half_ref[cw, :, ls] = lax.shift_right_arithmetic(key, 16).astype(jnp.int16)

    def idx_step(s, buf):
        idx_dots(jnp.minimum(s, n_chunks - 1), buf)
        idx_reduce(jnp.clip(s - 1, 0, n_chunks - 1), (s >= 1) & (s <= n_chunks), 1 - buf)

    def idx_two_steps(t, carry):
        idx_step(2 * t, 0)
        idx_step(2 * t + 1, 1)
        return carry

    lax.fori_loop(0, (n_chunks + 2) // 2, idx_two_steps, 0)

    k_row = jnp.minimum(t_idx + 1, TOPK_MAX)
    I16_MIN = -(2 ** 15)

    def count(pred):
        def body(c, acc):
            return acc + jnp.where(pred(c, keys_ref[c]), 1, 0).astype(jnp.int32)
        acc = lax.fori_loop(0, n_chunks, body, jnp.zeros((tk, tq), jnp.int32))
        return jnp.sum(acc, axis=0, keepdims=True)

    one16 = jnp.ones((tk, tq), jnp.int16)
    zero16 = jnp.zeros((tk, tq), jnp.int16)

    def count16(pred):
        def body(c2, acc):
            acc = acc + jnp.where(pred(half_ref[2 * c2]), one16, zero16)
            return acc + jnp.where(pred(half_ref[2 * c2 + 1]), one16, zero16)
        acc = lax.fori_loop(0, n_chunks // 2, body, jnp.zeros((tk, tq), jnp.int16))
        return jnp.sum(acc.astype(jnp.int32), axis=0, keepdims=True)

    def kth_largest16(k_need):
        def bit_round(r, prefix):
            cand = prefix + lax.shift_left(jnp.int32(1), 15 - r)
            cand16 = jnp.broadcast_to(cand, (tk, tq)).astype(jnp.int16)
            return jnp.where(count16(lambda v: v >= cand16) >= k_need, cand, prefix)
        return lax.fori_loop(0, 16, bit_round, jnp.full((1, tq), I16_MIN, jnp.int32))

    v_hi = kth_largest16(k_row)
    v_hi16 = jnp.broadcast_to(v_hi, (tk, tq)).astype(jnp.int16)
    k_low = k_row - count16(lambda v: v > v_hi16)

    def low_halves(c, carry):
        key = keys_ref[c]
        low = (key & jnp.int32(0xFFFF)) + I16_MIN
        in_bucket = lax.shift_right_arithmetic(key, 16) == v_hi
        half_ref[c] = jnp.where(in_bucket, low, I16_MIN).astype(jnp.int16)
        return carry
    lax.fori_loop(0, n_chunks, low_halves, 0)

    v_lo = kth_largest16(k_low)
    vstar = lax.shift_left(v_hi, 16) + (v_lo - I16_MIN)

    n_ge = count(lambda c, k: k >= vstar)

    @pl.when(jnp.max(n_ge - k_row) > 0)
    def _():
        need = k_row - count(lambda c, k: k > vstar)

        def idx_round(r, x):
            cand = x + lax.shift_left(jnp.int32(1), index_bits - 1 - r)

            def tie_below(c, k):
                s_idx = c * tk + lax.broadcasted_iota(jnp.int32, (tk, tq), 0)
                return (k == vstar) & (s_idx < cand)
            return jnp.where(count(tie_below) < need, cand, x)

        last = lax.fori_loop(0, index_bits, idx_round, jnp.zeros((1, tq), jnp.int32))

        def drop(c, carry):
            k = keys_ref[c]
            s_idx = c * tk + lax.broadcasted_iota(jnp.int32, (tk, tq), 0)
            keys_ref[c] = jnp.where((k == vstar) & (s_idx > last), vstar - 1, k)
            return carry
        lax.fori_loop(0, n_chunks, drop, 0)

    def to_mask(c, carry):
        maskb = jnp.where(keys_ref[c] >= vstar, 0.0, NEG_BIG)
        keys_ref[c] = lax.bitcast_convert_type(maskb, jnp.int32)
        return carry
    lax.fori_loop(0, n_chunks, to_mask, 0)
    keys_ref[dummy_chunk] = lax.bitcast_convert_type(jnp.full((tk, tq), NEG_BIG, jnp.float32), jnp.int32)

    m_ref[...] = jnp.full(m_ref.shape, NEG_BIG, jnp.float32)
    l_ref[...] = jnp.zeros(l_ref.shape, jnp.float32)
    acc_ref[...] = jnp.zeros(acc_ref.shape, jnp.float32)
    p_ref[...] = jnp.zeros(p_ref.shape, p_ref.dtype)
    alpha_ref[...] = jnp.ones(alpha_ref.shape, jnp.float32)
    first_near = (tq // tk) * i - 1

    def pv_stage(c, buf):
        for h in range(N_HEADS_A):
            r0 = HEAD_DIM_A * h
            pv = jnp.dot(vt_ref[0, c, r0:r0 + HEAD_DIM_A, :], p_ref[buf, h],
                         preferred_element_type=jnp.float32)
            acc_ref[r0:r0 + HEAD_DIM_A, :] = alpha_ref[buf, h:h + 1, :] * acc_ref[r0:r0 + HEAD_DIM_A, :] + pv

    def qk_stage(c, buf):
        for j in range(n_pairs):
            d = jnp.dot(ka_ref[0, c, :, LANES * j:LANES * (j + 1)], qat_ref[0, 0, j],
                        preferred_element_type=jnp.float32)
            lg_ref[buf, 2 * j] = d[:, :tq]
            lg_ref[buf, 2 * j + 1] = d[:, tq:]

    def softmax_stage(c, valid, buf, near):
        cm = jnp.where(valid, c, dummy_chunk)
        tbl = jnp.where(valid, jnp.clip(c - first_near + 1, 0, 3), 0)
        for h in range(N_HEADS_A):
            for hf in range(tq // LANES):
                ls = slice(hf * LANES, (hf + 1) * LANES)
                lm = lg_ref[buf, h, :, ls] + lax.bitcast_convert_type(keys_ref[cm, :, ls], jnp.float32)
                if near:
                    lm = lm + bias_ref[tbl, h, :, ls]
                m_old = m_ref[h:h + 1, ls]
                m_new = jnp.maximum(m_old, jnp.max(lm, axis=0, keepdims=True))
                alpha = jnp.exp(m_old - m_new)
                p = jnp.exp(lm - m_new)
                l_ref[h:h + 1, ls] = alpha * l_ref[h:h + 1, ls] + jnp.sum(p, axis=0, keepdims=True)
                m_ref[h:h + 1, ls] = m_new
                alpha_ref[buf, h:h + 1, ls] = alpha
                p_ref[buf, h, :, ls] = p.astype(p_ref.dtype)

    def step(s, buf, near):
        qk_stage(jnp.minimum(s, n_chunks - 1), buf)
        pv_stage(jnp.maximum(s - 2, 0), buf)
        sm = s - 1
        softmax_stage(jnp.clip(sm, 0, n_chunks - 1), (sm >= 0) & (sm < n_chunks), 1 - buf, near)

    def two_steps(near):
        def body(t, carry):
            step(2 * t, 0, near)
            step(2 * t + 1, 1, near)
            return carry
        return body

    n_pairs_of_steps = (n_chunks + 2) // 2
    lax.fori_loop(0, n_pairs_of_steps - 2, two_steps(False), 0)
    lax.fori_loop(n_pairs_of_steps - 2, n_pairs_of_steps, two_steps(True), 0)

    for h in range(N_HEADS_A):
        r0 = HEAD_DIM_A * h
        inv = 1.0 / l_ref[h:h + 1, :]
        acc_ref[r0:r0 + HEAD_DIM_A, :] = acc_ref[r0:r0 + HEAD_DIM_A, :] * inv
    out_ref[0] = acc_ref[...].T.astype(out_ref.dtype)


def _bias_tiles(rel_bias):
    tq, tk = ATT_TQ, ATT_TK
    lo = -(tq - 1)
    dist = np.arange(lo, tk + tq)
    n = np.maximum(dist, 0)
    max_exact = N_BUCKETS // 2
    nf = np.maximum(n, max_exact).astype(np.float64)
    large = max_exact + (np.log(nf / max_exact) / math.log(MAX_DISTANCE / max_exact)
                         * (N_BUCKETS - max_exact)).astype(np.int32)
    bucket = np.where(n < max_exact, n, np.minimum(large, N_BUCKETS - 1)).astype(np.int32)
    vec = (rel_bias[bucket] - rel_bias[N_BUCKETS - 1][None, :]).T
    tiles = [jnp.zeros((N_HEADS_A, tk, tq), rel_bias.dtype)]
    for base in (tk, 0, -tk):
        rows = [lax.slice_in_dim(vec, base - r - lo, base - r - lo + tq, axis=1) for r in range(tk)]
        tiles.append(jnp.stack(rows, axis=1))
    return jnp.stack(tiles)


def _sparse_attention(kidx4, ka4, vt4, qit2, qat2, wt, bias_tiles):
    b, n_c = kidx4.shape[0], kidx4.shape[1]
    seq = n_c * ATT_TK
    tq = ATT_TQ
    n_pairs = N_HEADS_A // 2
    kern = lambda *refs: _attn_kernel(*refs, index_bits=max(1, (seq - 1).bit_length()))
    return pl.pallas_call(
        kern,
        grid=(b, seq // tq),
        in_specs=[
            _resident((1, n_c, ATT_TK, LANES), lambda bi, i: (bi, 0, 0, 0)),
            _resident((1, n_c, ATT_TK, D_A), lambda bi, i: (bi, 0, 0, 0)),
            _resident((1, n_c, D_A, ATT_TK), lambda bi, i: (bi, 0, 0, 0)),
            pl.BlockSpec((1, 1, N_IDX_HEADS // 2, LANES, 2 * tq), lambda bi, i: (bi, i, 0, 0, 0)),
            pl.BlockSpec((1, 1, n_pairs, LANES, 2 * tq), lambda bi, i: (bi, i, 0, 0, 0)),
            pl.BlockSpec((N_IDX_HEADS, tq), lambda bi, i: (0, bi * (seq // tq) + i)),
            _resident((4, N_HEADS_A, ATT_TK, tq), lambda bi, i: (0, 0, 0, 0)),
        ],
        out_specs=pl.BlockSpec((1, tq, D_A), lambda bi, i: (bi, i, 0)),
        out_shape=jax.ShapeDtypeStruct((b, seq, D_A), jnp.bfloat16),
        scratch_shapes=[
            pltpu.VMEM((n_c + 1, ATT_TK, tq), jnp.int32),
            pltpu.VMEM((n_c + 1, ATT_TK, tq), jnp.int16),
            pltpu.VMEM((2, N_HEADS_A, ATT_TK, tq), jnp.float32),
            pltpu.VMEM((2, N_HEADS_A, ATT_TK, tq), jnp.bfloat16),
            pltpu.VMEM((2, N_HEADS_A, tq), jnp.float32),
            pltpu.VMEM((N_HEADS_A, tq), jnp.float32),
            pltpu.VMEM((N_HEADS_A, tq), jnp.float32),
            pltpu.VMEM((D_A, tq), jnp.float32),
        ],
        compiler_params=pltpu.CompilerParams(
            dimension_semantics=("arbitrary", "arbitrary"), vmem_limit_bytes=VMEM_LIMIT),
        name="indexer_topk_attention",
    )(kidx4, ka4, vt4, qit2, qat2, wt, bias_tiles)


def _ret_kernel(q_ref, k_ref, v_ref, g_ref, cos_ref, sin_ref, dec_ref, qdec_ref, kdec_ref,
                cdec_ref, gn_ref, out_ref, state_ref):
    @pl.when(pl.program_id(1) == 0)
    def _():
        state_ref[...] = jnp.zeros(state_ref.shape, jnp.float32)

    cos = cos_ref[...]
    sin = sin_ref[...]
    half = HEAD_DIM_B // 2
    for h in range(N_HEADS_B):
        lo = HEAD_DIM_B * h
        q = q_ref[0, :, lo:lo + HEAD_DIM_B]
        k = k_ref[0, :, lo:lo + HEAD_DIM_B]
        qr = q * cos + pltpu.roll(q, half, 1) * sin
        ks = (k * cos + pltpu.roll(k, half, 1) * sin) * (HEAD_DIM_B ** -0.5)
        qb = qr.astype(jnp.bfloat16)
        vb = v_ref[0, :, lo:lo + HEAD_DIM_B].astype(jnp.bfloat16)
        qk = lax.dot_general(qb, ks.astype(jnp.bfloat16), (((1,), (1,)), ((), ())),
                             preferred_element_type=jnp.float32)
        inner = (qk * dec_ref[h]).astype(jnp.bfloat16)
        state = state_ref[h]
        y = (jnp.dot(inner, vb, preferred_element_type=jnp.float32)
             + jnp.dot(qb, state.astype(jnp.bfloat16), preferred_element_type=jnp.float32) * qdec_ref[h])
        kd_t = (ks * kdec_ref[h]).T.astype(jnp.bfloat16)
        state_ref[h] = state * cdec_ref[h] + jnp.dot(kd_t, vb, preferred_element_type=jnp.float32)
        yn = y * lax.rsqrt(jnp.mean(y * y, axis=-1, keepdims=True) + EPS) * gn_ref[:, lo:lo + HEAD_DIM_B]
        g = g_ref[0, :, lo:lo + HEAD_DIM_B]
        out_ref[0, :, lo:lo + HEAD_DIM_B] = (yn * (g * jax.nn.sigmoid(g))).astype(out_ref.dtype)


def _retention_tables(seq):
    c = RET_CHUNK
    half = HEAD_DIM_B // 2
    theta = 1.0 / (RET_THETA_BASE ** jnp.linspace(0.0, 1.0, half, dtype=jnp.float32))
    ang = jnp.arange(seq, dtype=jnp.int32).astype(jnp.float32)[:, None] * theta[None, :]
    cos = jnp.concatenate([jnp.cos(ang), jnp.cos(ang)], axis=-1)
    sin = jnp.concatenate([-jnp.sin(ang), jnp.sin(ang)], axis=-1)
    gamma = 1.0 - 2.0 ** (-5.0 - jnp.arange(N_HEADS_B, dtype=jnp.float32))
    log_g = jnp.log(gamma)
    pos = jnp.arange(c, dtype=jnp.float32)
    diff = pos[:, None] - pos[None, :]
    dec = jnp.where(diff >= 0, jnp.exp(log_g[:, None, None] * jnp.maximum(diff, 0.0)), 0.0)
    qdec = jnp.exp(log_g[:, None] * (pos[None, :] + 1.0))
    kdec = jnp.exp(log_g[:, None] * (c - 1.0 - pos[None, :]))
    cdec = jnp.exp(log_g * c)
    bc = lambda a: jnp.broadcast_to(a[:, :, None], (N_HEADS_B, c, HEAD_DIM_B))
    cdec_b = jnp.broadcast_to(cdec[:, None, None], (N_HEADS_B, 1, HEAD_DIM_B))
    return cos, sin, dec, bc(qdec), bc(kdec), cdec_b


def _retention(ob3, gn):
    b, seq, _ = ob3.shape
    c = RET_CHUNK
    cos, sin, dec, qdec, kdec, cdec = _retention_tables(seq)
    col = lambda j: (lambda bi, ci: (bi, ci, j))
    const3 = lambda bi, ci: (0, 0, 0)
    return pl.pallas_call(
        _ret_kernel,
        grid=(b, seq // c),
        in_specs=[
            pl.BlockSpec((1, c, D_B), col(0)),
            pl.BlockSpec((1, c, D_B), col(1)),
            pl.BlockSpec((1, c, D_B), col(2)),
            pl.BlockSpec((1, c, D_B), col(3)),
            pl.BlockSpec((c, HEAD_DIM_B), lambda bi, ci: (ci, 0)),
            pl.BlockSpec((c, HEAD_DIM_B), lambda bi, ci: (ci, 0)),
            _resident((N_HEADS_B, c, c), const3),
            _resident((N_HEADS_B, c, HEAD_DIM_B), const3),
            _resident((N_HEADS_B, c, HEAD_DIM_B), const3),
            _resident((N_HEADS_B, 1, HEAD_DIM_B), const3),
            _resident((1, D_B), lambda bi, ci: (0, 0)),
        ],
        out_specs=pl.BlockSpec((1, c, D_B), lambda bi, ci: (bi, ci, 0)),
        out_shape=jax.ShapeDtypeStruct((b, seq, D_B), jnp.bfloat16),
        scratch_shapes=[pltpu.VMEM((N_HEADS_B, HEAD_DIM_B, HEAD_DIM_B), jnp.float32)],
        compiler_params=pltpu.CompilerParams(
            dimension_semantics=("arbitrary", "arbitrary"), vmem_limit_bytes=VMEM_LIMIT),
        name="retention",
    )(ob3, ob3, ob3, ob3, cos, sin, dec, qdec, kdec, cdec, gn)


def _ffn_kernel(x_ref, a_ref, b_ref, woa_ref, wob_ref, gffn_ref, wg_ref, wu_ref, wd_ref, gfin_ref,
                out_ref):
    x1 = (x_ref[...]
          + jnp.dot(a_ref[...], woa_ref[...], preferred_element_type=jnp.float32)
          + jnp.dot(b_ref[...], wob_ref[...], preferred_element_type=jnp.float32))
    h2 = (x1 * lax.rsqrt(jnp.mean(x1 * x1, axis=-1, keepdims=True) + EPS) * gffn_ref[...]).astype(jnp.bfloat16)
    gate = jnp.dot(h2, wg_ref[...], preferred_element_type=jnp.float32)
    up = jnp.dot(h2, wu_ref[...], preferred_element_type=jnp.float32)
    u = (gate * jax.nn.sigmoid(gate) * up).astype(jnp.bfloat16)
    x2 = x1 + jnp.dot(u, wd_ref[...], preferred_element_type=jnp.float32)
    out_ref[...] = x2 * lax.rsqrt(jnp.mean(x2 * x2, axis=-1, keepdims=True) + EPS) * gfin_ref[...]


def _out_ffn(x2d, a2d, b2d, woa, wob, gffn, wg, wu, wd, gfin):
    rows, d_model = x2d.shape
    d_ff = wg.shape[1]
    tm = FFN_ROWS
    const = lambda i: (0, 0)
    return pl.pallas_call(
        _ffn_kernel,
        grid=(rows // tm,),
        in_specs=[
            pl.BlockSpec((tm, d_model), lambda i: (i, 0)),
            pl.BlockSpec((tm, D_A), lambda i: (i, 0)),
            pl.BlockSpec((tm, D_B), lambda i: (i, 0)),
            _resident((D_A, d_model), const),
            _resident((D_B, d_model), const),
            _resident((1, d_model), const),
            _resident((d_model, d_ff), const),
            _resident((d_model, d_ff), const),
            _resident((d_ff, d_model), const),
            _resident((1, d_model), const),
        ],
        out_specs=pl.BlockSpec((tm, d_model), lambda i: (i, 0)),
        out_shape=jax.ShapeDtypeStruct((rows, d_model), jnp.float32),
        compiler_params=pltpu.CompilerParams(
            dimension_semantics=("arbitrary",), vmem_limit_bytes=VMEM_LIMIT),
        name="out_proj_swiglu",
    )(x2d, a2d, b2d, woa, wob, gffn, wg, wu, wd, gfin)


def kernel(x, norm_mix_g, w_in, idx_k_ln_g, idx_k_ln_b, rel_bias, ret_gn_g, w_out, norm_ffn_g,
           w_gate, w_up, w_down, norm_final_g):
    b, seq, d_model = x.shape
    assert seq % ATT_TQ == 0 and seq % RET_CHUNK == 0 and (b * seq) % PROJ_ROWS == 0
    assert norm_mix_g.shape[0] == 1, "single-layer block"
    bf = jnp.bfloat16
    rows = b * seq
    n_c = seq // ATT_TK
    o = 0
    w = w_in[0]
    qa_w = w[:, o:o + D_A] * (HEAD_DIM_A ** -0.5); o += D_A
    ka_w = w[:, o:o + D_A]; o += D_A
    va_w = w[:, o:o + D_A]; o += D_A
    qi_w = w[:, o:o + N_IDX_HEADS * IDX_DIM] * (IDX_DIM ** -0.5); o += N_IDX_HEADS * IDX_DIM
    small_w = w[:, o:o + IDX_DIM + N_IDX_HEADS]; o += IDX_DIM + N_IDX_HEADS
    b_w = w[:, o:]
    wa = jnp.concatenate([qa_w, ka_w, va_w, qi_w], axis=1).astype(bf)
    ws = jnp.pad(small_w, ((0, 0), (0, LANES - small_w.shape[1]))).astype(bf)
    wb = b_w.astype(bf)
    lng = jnp.pad(idx_k_ln_g[0], (0, LANES - IDX_DIM)).reshape(1, LANES)
    lnb = jnp.pad(idx_k_ln_b[0], (0, LANES - IDX_DIM)).reshape(1, LANES)

    x2d = x.reshape(rows, d_model)
    qat, qit, ka, vt, kidx, wt, ob = _project(x2d, norm_mix_g[0].reshape(1, d_model), wa, ws, wb, lng, lnb)

    n_q, n_pairs = seq // ATT_TQ, N_HEADS_A // 2
    out_a = _sparse_attention(
        kidx.reshape(b, n_c, ATT_TK, LANES), ka.reshape(b, n_c, ATT_TK, D_A),
        vt.reshape(b, n_c, D_A, ATT_TK), qit.reshape(b, n_q, n_pairs, LANES, 2 * ATT_TQ),
        qat.reshape(b, n_q, n_pairs, LANES, 2 * ATT_TQ), wt, _bias_tiles(rel_bias)).reshape(rows, D_A)

    out_b = _retention(ob.reshape(b, seq, 4 * D_B), ret_gn_g[0].reshape(1, D_B)).reshape(rows, D_B)

    wo = w_out[0]
    out = _out_ffn(x2d, out_a, out_b, wo[:D_A].astype(bf), wo[D_A:].astype(bf),
                   norm_ffn_g[0].reshape(1, d_model), w_gate[0].astype(bf), w_up[0].astype(bf),
                   w_down[0].astype(bf), norm_final_g.reshape(1, d_model))
    return out.reshape(b, seq, d_model)
```

```python
import math

import jax
import jax.numpy as jnp
import numpy as np
from jax import lax
from jax.experimental import pallas as pl
from jax.experimental.pallas import tpu as pltpu

N_HEADS_A = 8
HEAD_DIM_A = 64
N_IDX_HEADS = 8
IDX_DIM = 64
TOPK_MAX = 256
N_HEADS_B = 4
HEAD_DIM_B = 128
RET_THETA_BASE = 10000.0
N_BUCKETS = 32
MAX_DISTANCE = 128
EPS = 1e-6

D_A = N_HEADS_A * HEAD_DIM_A
D_B = N_HEADS_B * HEAD_DIM_B
BF16_ROWS = 16
V_ROWS = HEAD_DIM_A + BF16_ROWS
LOG2E = math.log2(math.e)

LANES = 128
MXU_COLS = 256

PROJ_ROWS = 512
ATT_TQ = MXU_COLS
ATT_TK = LANES
COUNT_UNROLL = 8
RET_CHUNK = 256
FFN_ROWS = 256
VMEM_LIMIT = 56 * 1024 * 1024

NEG_BIG = -1e30
INT_MIN = -(2 ** 31)
assert ATT_TQ == 2 * ATT_TK


def _resident(block_shape, index_map):
    return pl.BlockSpec(block_shape, index_map, pipeline_mode=pl.Buffered(1))


def _proj_kernel(x_ref, g_ref, wa_ref, ws_ref, wb_ref, lng_ref, lnb_ref,
                 qat_ref, qit_ref, ka_ref, vt_ref, okidx_ref, owt_ref, ob_ref):
    bf = jnp.bfloat16
    x = x_ref[...]
    tm = x.shape[0]
    h = x * lax.rsqrt(jnp.mean(x * x, axis=-1, keepdims=True) + EPS) * g_ref[...]
    hb = h.astype(bf)

    def part(k):
        return jnp.dot(hb, wa_ref[:, k * D_A:(k + 1) * D_A], preferred_element_type=jnp.float32)

    ka_ref[...] = part(1).astype(bf)
    va = part(2)
    row = lax.broadcasted_iota(jnp.int32, (V_ROWS - HEAD_DIM_A, ATT_TK), 0)
    ones_rows = jnp.where(row == 0, 1.0, 0.0).astype(bf)
    for r in range(tm // ATT_TK):
        v_t = va[r * ATT_TK:(r + 1) * ATT_TK, :].T.astype(bf)
        for hh in range(N_HEADS_A):
            vt_ref[r, hh * V_ROWS:hh * V_ROWS + HEAD_DIM_A, :] = v_t[hh * HEAD_DIM_A:(hh + 1) * HEAD_DIM_A]
            vt_ref[r, hh * V_ROWS + HEAD_DIM_A:(hh + 1) * V_ROWS, :] = ones_rows
    qa = part(0)
    qi = part(3)
    hd, tq = HEAD_DIM_A, ATT_TQ
    zero_q = jnp.zeros((hd, tq), bf)
    for qb in range(tm // tq):
        qat = qa[qb * tq:(qb + 1) * tq, :].T.astype(bf)
        qit = qi[qb * tq:(qb + 1) * tq, :].T.astype(bf)
        for j in range(N_HEADS_A // 2):
            even = slice(2 * j * hd, (2 * j + 1) * hd)
            odd = slice((2 * j + 1) * hd, (2 * j + 2) * hd)
            qat_ref[qb, j, 0:hd, 0:tq] = qat[even]
            qat_ref[qb, j, hd:2 * hd, 0:tq] = zero_q
            qat_ref[qb, j, 0:hd, tq:2 * tq] = zero_q
            qat_ref[qb, j, hd:2 * hd, tq:2 * tq] = qat[odd]
            qit_ref[qb, j, 0:hd, 0:tq] = qit[even]
            qit_ref[qb, j, 0:hd, tq:2 * tq] = qit[odd]
            qit_ref[qb, j, hd:2 * hd, 0:tq] = zero_q
            qit_ref[qb, j, hd:2 * hd, tq:2 * tq] = zero_q
    ob_ref[...] = jnp.dot(hb, wb_ref[...], preferred_element_type=jnp.float32)
    s = jnp.dot(hb, ws_ref[...], preferred_element_type=jnp.float32)
    lane = lax.broadcasted_iota(jnp.int32, s.shape, 1)
    is_k = lane < IDX_DIM
    mu = jnp.sum(jnp.where(is_k, s, 0.0), axis=-1, keepdims=True) * (1.0 / IDX_DIM)
    d = jnp.where(is_k, s - mu, 0.0)
    var = jnp.sum(d * d, axis=-1, keepdims=True) * (1.0 / IDX_DIM)
    kn = d * lax.rsqrt(var + EPS) * lng_ref[...] + lnb_ref[...]
    okidx_ref[...] = jnp.where(is_k, kn, 0.0).astype(okidx_ref.dtype)
    owt_ref[...] = (s * (N_IDX_HEADS ** -0.5)).T[IDX_DIM:IDX_DIM + N_IDX_HEADS, :]


def _project(x2, g, wa, ws, wb, lng, lnb):
    rows, d_model = x2.shape
    tm = PROJ_ROWS
    na, nb = wa.shape[1], wb.shape[1]
    n_pairs = N_HEADS_A // 2
    const = lambda i: (0, 0)
    return pl.pallas_call(
        _proj_kernel,
        grid=(rows // tm,),
        in_specs=[
            pl.BlockSpec((tm, d_model), lambda i: (i, 0)),
            _resident((1, d_model), const),
            _resident((d_model, na), const),
            _resident((d_model, LANES), const),
            _resident((d_model, nb), const),
            _resident((1, LANES), const),
            _resident((1, LANES), const),
        ],
        out_specs=[
            pl.BlockSpec((tm // ATT_TQ, n_pairs, LANES, 2 * ATT_TQ), lambda i: (i, 0, 0, 0)),
            pl.BlockSpec((tm // ATT_TQ, n_pairs, LANES, 2 * ATT_TQ), lambda i: (i, 0, 0, 0)),
            pl.BlockSpec((tm, D_A), lambda i: (i, 0)),
            pl.BlockSpec((tm // ATT_TK, N_HEADS_A * V_ROWS, ATT_TK), lambda i: (i, 0, 0)),
            pl.BlockSpec((tm, LANES), lambda i: (i, 0)),
            pl.BlockSpec((N_IDX_HEADS, tm), lambda i: (0, i)),
            pl.BlockSpec((tm, nb), lambda i: (i, 0)),
        ],
        out_shape=[
            jax.ShapeDtypeStruct((rows // ATT_TQ, n_pairs, LANES, 2 * ATT_TQ), jnp.bfloat16),
            jax.ShapeDtypeStruct((rows // ATT_TQ, n_pairs, LANES, 2 * ATT_TQ), jnp.bfloat16),
            jax.ShapeDtypeStruct((rows, D_A), jnp.bfloat16),
            jax.ShapeDtypeStruct((rows // ATT_TK, N_HEADS_A * V_ROWS, ATT_TK), jnp.bfloat16),
            jax.ShapeDtypeStruct((rows, LANES), jnp.bfloat16),
            jax.ShapeDtypeStruct((N_IDX_HEADS, rows), jnp.float32),
            jax.ShapeDtypeStruct((rows, nb), jnp.float32),
        ],
        compiler_params=pltpu.CompilerParams(
            dimension_semantics=("arbitrary",), vmem_limit_bytes=VMEM_LIMIT),
        name="rmsnorm_in_proj",
    )(x2, g, wa, ws, wb, lng, lnb)


def _attn_kernel(kidx_ref, ka_ref, vt_ref, qit_ref, qat_ref, wt_ref, bias_ref, out_ref,
                 keys_ref, half_ref, lg_ref, p_ref, alpha_ref, m_ref, acc_ref, outt_ref, *, index_bits):
    i = pl.program_id(1)
    tq, tk = ATT_TQ, ATT_TK
    n_pairs = N_HEADS_A // 2
    n_chunks = (i + 1) * (tq // tk)
    dummy_chunk = keys_ref.shape[0] - 1
    t_idx = i * tq + lax.broadcasted_iota(jnp.int32, (1, tq), 1)

    lg_ref[...] = jnp.zeros(lg_ref.shape, jnp.float32)

    def idx_dots(c, buf):
        kc = kidx_ref[0, c]
        for j in range(N_IDX_HEADS // 2):
            d = jnp.dot(kc, qit_ref[0, 0, j], preferred_element_type=jnp.float32)
            lg_ref[buf, 2 * j] = d[:, :tq]
            lg_ref[buf, 2 * j + 1] = d[:, tq:]

    def idx_reduce(c, valid, buf):
        cw = jnp.where(valid, c, dummy_chunk)
        for hf in range(tq // LANES):
            ls = slice(hf * LANES, (hf + 1) * LANES)
            acc = jnp.zeros((tk, LANES), jnp.float32)
            for h in range(N_IDX_HEADS):
                acc = acc + wt_ref[h:h + 1, ls] * jnp.maximum(lg_ref[buf, h, :, ls], 0.0)
            bits = lax.bitcast_convert_type(acc + 0.0, jnp.int32)
            key = jnp.where(bits < 0, bits ^ jnp.int32(0x7FFFFFFF), bits)
            s_idx = c * tk + lax.broadcasted_iota(jnp.int32, (tk, LANES), 0)
            key = jnp.where(s_idx <= t_idx[:, ls], key, jnp.int32(INT_MIN))
            keys_ref[cw, :, ls] = key
            half_ref[cw, :, ls] = lax.shift_right_arithmetic(key, 16).astype(jnp.int16)

    def idx_step(s, buf):
        idx_dots(jnp.minimum(s, n_chunks - 1), buf)
        idx_reduce(jnp.clip(s - 1, 0, n_chunks - 1), (s >= 1) & (s <= n_chunks), 1 - buf)

    def idx_two_steps(t, carry):
        idx_step(2 * t, 0)
        idx_step(2 * t + 1, 1)
        return carry

    lax.fori_loop(0, (n_chunks + 2) // 2, idx_two_steps, 0)

    k_row = jnp.minimum(t_idx + 1, TOPK_MAX)
    I16_MIN = -(2 ** 15)

    def count(pred):
        def body(c, acc):
            return acc + jnp.where(pred(c, keys_ref[c]), 1, 0).astype(jnp.int32)
        acc = lax.fori_loop(0, n_chunks, body, jnp.zeros((tk, tq), jnp.int32))
        return jnp.sum(acc, axis=0, keepdims=True)

    one16 = jnp.ones((tk, tq), jnp.int16)
    zero16 = jnp.zeros((tk, tq), jnp.int16)

    def count16(pred):
        def chunks(first, n):
            def body(g, acc):
                for u in range(n):
                    acc = acc + jnp.where(pred(half_ref[first + n * g + u]), one16, zero16)
                return acc
            return body
        n_long = n_chunks // COUNT_UNROLL
        acc = lax.fori_loop(0, n_long, chunks(0, COUNT_UNROLL), jnp.zeros((tk, tq), jnp.int16))
        acc = lax.fori_loop(0, (n_chunks - COUNT_UNROLL * n_long) // 2,
                            chunks(COUNT_UNROLL * n_long, 2), acc)
        return jnp.sum(acc.astype(jnp.int32), axis=0, keepdims=True)

    def kth_largest16(k_need):
        def bit_round(r, prefix):
            cand = prefix + lax.shift_left(jnp.int32(1), 15 - r)
            cand16 = jnp.broadcast_to(cand, (tk, tq)).astype(jnp.int16)
            return jnp.where(count16(lambda v: v >= cand16) >= k_need, cand, prefix)
        return lax.fori_loop(0, 16, bit_round, jnp.full((1, tq), I16_MIN, jnp.int32))

    v_hi = kth_largest16(k_row)
    v_hi16 = jnp.broadcast_to(v_hi, (tk, tq)).astype(jnp.int16)
    k_low = k_row - count16(lambda v: v > v_hi16)

    def low_halves(c, carry):
        key = keys_ref[c]
        low = (key & jnp.int32(0xFFFF)) + I16_MIN
        in_bucket = lax.shift_right_arithmetic(key, 16) == v_hi
        half_ref[c] = jnp.where(in_bucket, low, I16_MIN).astype(jnp.int16)
        return carry
    lax.fori_loop(0, n_chunks, low_halves, 0)

    v_lo = kth_largest16(k_low)
    vstar = lax.shift_left(v_hi, 16) + (v_lo - I16_MIN)

    n_ge = count(lambda c, k: k >= vstar)

    @pl.when(jnp.max(n_ge - k_row) > 0)
    def _():
        need = k_row - count(lambda c, k: k > vstar)

        def idx_round(r, x):
            cand = x + lax.shift_left(jnp.int32(1), index_bits - 1 - r)

            def tie_below(c, k):
                s_idx = c * tk + lax.broadcasted_iota(jnp.int32, (tk, tq), 0)
                return (k == vstar) & (s_idx < cand)
            return jnp.where(count(tie_below) < need, cand, x)

        last = lax.fori_loop(0, index_bits, idx_round, jnp.zeros((1, tq), jnp.int32))

        def drop(c, carry):
            k = keys_ref[c]
            s_idx = c * tk + lax.broadcasted_iota(jnp.int32, (tk, tq), 0)
            keys_ref[c] = jnp.where((k == vstar) & (s_idx > last), vstar - 1, k)
            return carry
        lax.fori_loop(0, n_chunks, drop, 0)

    def to_mask(c, carry):
        maskb = jnp.where(keys_ref[c] >= vstar, 0.0, NEG_BIG)
        keys_ref[c] = lax.bitcast_convert_type(maskb, jnp.int32)
        return carry
    lax.fori_loop(0, n_chunks, to_mask, 0)
    keys_ref[dummy_chunk] = lax.bitcast_convert_type(jnp.full((tk, tq), NEG_BIG, jnp.float32), jnp.int32)

    m_ref[...] = jnp.full(m_ref.shape, NEG_BIG, jnp.float32)
    acc_ref[...] = jnp.zeros(acc_ref.shape, jnp.float32)
    p_ref[...] = jnp.zeros(p_ref.shape, p_ref.dtype)
    alpha_ref[...] = jnp.ones(alpha_ref.shape, jnp.float32)
    first_near = (tq // tk) * i - 1

    def pv_stage(c, buf):
        for h in range(N_HEADS_A):
            rs = slice(V_ROWS * h, V_ROWS * (h + 1))
            pv = jnp.dot(vt_ref[0, c, rs, :], p_ref[buf, h], preferred_element_type=jnp.float32)
            acc_ref[rs, :] = alpha_ref[buf, h:h + 1, :] * acc_ref[rs, :] + pv

    def qk_stage(c, buf):
        for j in range(n_pairs):
            d = jnp.dot(ka_ref[0, c, :, LANES * j:LANES * (j + 1)], qat_ref[0, 0, j],
                        preferred_element_type=jnp.float32)
            lg_ref[buf, 2 * j] = d[:, :tq]
            lg_ref[buf, 2 * j + 1] = d[:, tq:]

    def softmax_stage(c, valid, buf, near):
        cm = jnp.where(valid, c, dummy_chunk)
        tbl = jnp.where(valid, jnp.clip(c - first_near + 1, 0, 3), 0)
        for h in range(N_HEADS_A):
            for hf in range(tq // LANES):
                ls = slice(hf * LANES, (hf + 1) * LANES)
                lm = lg_ref[buf, h, :, ls] + lax.bitcast_convert_type(keys_ref[cm, :, ls], jnp.float32)
                if near:
                    lm = lm + bias_ref[tbl, h, :, ls]
                m_old = m_ref[h:h + 1, ls]
                m_new = jnp.maximum(m_old, jnp.max(lm, axis=0, keepdims=True))
                alpha = jnp.exp2(m_old - m_new)
                p = jnp.exp2(lm - m_new)
                m_ref[h:h + 1, ls] = m_new
                alpha_ref[buf, h:h + 1, ls] = alpha
                p_ref[buf, h, :, ls] = p.astype(p_ref.dtype)

    def step(s, buf, near):
        qk_stage(jnp.minimum(s, n_chunks - 1), buf)
        pv_stage(jnp.maximum(s - 2, 0), buf)
        sm = s - 1
        softmax_stage(jnp.clip(sm, 0, n_chunks - 1), (sm >= 0) & (sm < n_chunks), 1 - buf, near)

    def two_steps(near):
        def body(t, carry):
            step(2 * t, 0, near)
            step(2 * t + 1, 1, near)
            return carry
        return body

    n_pairs_of_steps = (n_chunks + 2) // 2
    lax.fori_loop(0, n_pairs_of_steps - 2, two_steps(False), 0)
    lax.fori_loop(n_pairs_of_steps - 2, n_pairs_of_steps, two_steps(True), 0)

    for h in range(N_HEADS_A):
        r0 = V_ROWS * h
        inv = 1.0 / acc_ref[r0 + HEAD_DIM_A:r0 + HEAD_DIM_A + 1, :]
        outt_ref[HEAD_DIM_A * h:HEAD_DIM_A * (h + 1), :] = acc_ref[r0:r0 + HEAD_DIM_A, :] * inv
    out_ref[0] = outt_ref[...].T.astype(out_ref.dtype)


def _bias_tiles(rel_bias):
    tq, tk = ATT_TQ, ATT_TK
    lo = -(tq - 1)
    dist = np.arange(lo, tk + tq)
    n = np.maximum(dist, 0)
    max_exact = N_BUCKETS // 2
    nf = np.maximum(n, max_exact).astype(np.float64)
    large = max_exact + (np.log(nf / max_exact) / math.log(MAX_DISTANCE / max_exact)
                         * (N_BUCKETS - max_exact)).astype(np.int32)
    bucket = np.where(n < max_exact, n, np.minimum(large, N_BUCKETS - 1)).astype(np.int32)
    vec = ((rel_bias[bucket] - rel_bias[N_BUCKETS - 1][None, :]) * LOG2E).T
    tiles = [jnp.zeros((N_HEADS_A, tk, tq), rel_bias.dtype)]
    for base in (tk, 0, -tk):
        rows = [lax.slice_in_dim(vec, base - r - lo, base - r - lo + tq, axis=1) for r in range(tk)]
        tiles.append(jnp.stack(rows, axis=1))
    return jnp.stack(tiles)


def _sparse_attention(kidx4, ka4, vt4, qit2, qat2, wt, bias_tiles):
    b, n_c = kidx4.shape[0], kidx4.shape[1]
    seq = n_c * ATT_TK
    tq = ATT_TQ
    n_pairs = N_HEADS_A // 2
    kern = lambda *refs: _attn_kernel(*refs, index_bits=max(1, (seq - 1).bit_length()))
    return pl.pallas_call(
        kern,
        grid=(b, seq // tq),
        in_specs=[
            _resident((1, n_c, ATT_TK, LANES), lambda bi, i: (bi, 0, 0, 0)),
            _resident((1, n_c, ATT_TK, D_A), lambda bi, i: (bi, 0, 0, 0)),
            _resident((1, n_c, N_HEADS_A * V_ROWS, ATT_TK), lambda bi, i: (bi, 0, 0, 0)),
            pl.BlockSpec((1, 1, N_IDX_HEADS // 2, LANES, 2 * tq), lambda bi, i: (bi, i, 0, 0, 0)),
            pl.BlockSpec((1, 1, n_pairs, LANES, 2 * tq), lambda bi, i: (bi, i, 0, 0, 0)),
            pl.BlockSpec((N_IDX_HEADS, tq), lambda bi, i: (0, bi * (seq // tq) + i)),
            _resident((4, N_HEADS_A, ATT_TK, tq), lambda bi, i: (0, 0, 0, 0)),
        ],
        out_specs=pl.BlockSpec((1, tq, D_A), lambda bi, i: (bi, i, 0)),
        out_shape=jax.ShapeDtypeStruct((b, seq, D_A), jnp.bfloat16),
        scratch_shapes=[
            pltpu.VMEM((n_c + 1, ATT_TK, tq), jnp.int32),
            pltpu.VMEM((n_c + 1, ATT_TK, tq), jnp.int16),
            pltpu.VMEM((2, N_HEADS_A, ATT_TK, tq), jnp.float32),
            pltpu.VMEM((2, N_HEADS_A, ATT_TK, tq), jnp.bfloat16),
            pltpu.VMEM((2, N_HEADS_A, tq), jnp.float32),
            pltpu.VMEM((N_HEADS_A, tq), jnp.float32),
            pltpu.VMEM((N_HEADS_A * V_ROWS, tq), jnp.float32),
            pltpu.VMEM((D_A, tq), jnp.float32),
        ],
        compiler_params=pltpu.CompilerParams(
            dimension_semantics=("arbitrary", "arbitrary"), vmem_limit_bytes=VMEM_LIMIT),
        name="indexer_topk_attention",
    )(kidx4, ka4, vt4, qit2, qat2, wt, bias_tiles)


def _ret_kernel(q_ref, k_ref, v_ref, g_ref, cos_ref, sin_ref, dec_ref, qdec_ref, kdec_ref,
                cdec_ref, gn_ref, out_ref, state_ref):
    @pl.when(pl.program_id(1) == 0)
    def _():
        state_ref[...] = jnp.zeros(state_ref.shape, jnp.float32)

    cos = cos_ref[...]
    sin = sin_ref[...]
    half = HEAD_DIM_B // 2
    for h in range(N_HEADS_B):
        lo = HEAD_DIM_B * h
        q = q_ref[0, :, lo:lo + HEAD_DIM_B]
        k = k_ref[0, :, lo:lo + HEAD_DIM_B]
        qr = q * cos + pltpu.roll(q, half, 1) * sin
        ks = (k * cos + pltpu.roll(k, half, 1) * sin) * (HEAD_DIM_B ** -0.5)
        qb = qr.astype(jnp.bfloat16)
        vb = v_ref[0, :, lo:lo + HEAD_DIM_B].astype(jnp.bfloat16)
        qk = lax.dot_general(qb, ks.astype(jnp.bfloat16), (((1,), (1,)), ((), ())),
                             preferred_element_type=jnp.float32)
        inner = (qk * dec_ref[h]).astype(jnp.bfloat16)
        state = state_ref[h]
        y = (jnp.dot(inner, vb, preferred_element_type=jnp.float32)
             + jnp.dot(qb, state.astype(jnp.bfloat16), preferred_element_type=jnp.float32) * qdec_ref[h])
        kd_t = (ks * kdec_ref[h]).T.astype(jnp.bfloat16)
        state_ref[h] = state * cdec_ref[h] + jnp.dot(kd_t, vb, preferred_element_type=jnp.float32)
        yn = y * lax.rsqrt(jnp.mean(y * y, axis=-1, keepdims=True) + EPS) * gn_ref[:, lo:lo + HEAD_DIM_B]
        g = g_ref[0, :, lo:lo + HEAD_DIM_B]
        out_ref[0, :, lo:lo + HEAD_DIM_B] = (yn * (g * jax.nn.sigmoid(g))).astype(out_ref.dtype)


def _retention_tables(seq):
    c = RET_CHUNK
    half = HEAD_DIM_B // 2
    theta = 1.0 / (RET_THETA_BASE ** jnp.linspace(0.0, 1.0, half, dtype=jnp.float32))
    ang = jnp.arange(seq, dtype=jnp.int32).astype(jnp.float32)[:, None] * theta[None, :]
    cos = jnp.concatenate([jnp.cos(ang), jnp.cos(ang)], axis=-1)
    sin = jnp.concatenate([-jnp.sin(ang), jnp.sin(ang)], axis=-1)
    gamma = 1.0 - 2.0 ** (-5.0 - jnp.arange(N_HEADS_B, dtype=jnp.float32))
    log_g = jnp.log(gamma)
    pos = jnp.arange(c, dtype=jnp.float32)
    diff = pos[:, None] - pos[None, :]
    dec = jnp.where(diff >= 0, jnp.exp(log_g[:, None, None] * jnp.maximum(diff, 0.0)), 0.0)
    qdec = jnp.exp(log_g[:, None] * (pos[None, :] + 1.0))
    kdec = jnp.exp(log_g[:, None] * (c - 1.0 - pos[None, :]))
    cdec = jnp.exp(log_g * c)
    bc = lambda a: jnp.broadcast_to(a[:, :, None], (N_HEADS_B, c, HEAD_DIM_B))
    cdec_b = jnp.broadcast_to(cdec[:, None, None], (N_HEADS_B, 1, HEAD_DIM_B))
    return cos, sin, dec, bc(qdec), bc(kdec), cdec_b


def _retention(ob3, gn):
    b, seq, _ = ob3.shape
    c = RET_CHUNK
    cos, sin, dec, qdec, kdec, cdec = _retention_tables(seq)
    col = lambda j: (lambda bi, ci: (bi, ci, j))
    const3 = lambda bi, ci: (0, 0, 0)
    return pl.pallas_call(
        _ret_kernel,
        grid=(b, seq // c),
        in_specs=[
            pl.BlockSpec((1, c, D_B), col(0)),
            pl.BlockSpec((1, c, D_B), col(1)),
            pl.BlockSpec((1, c, D_B), col(2)),
            pl.BlockSpec((1, c, D_B), col(3)),
            pl.BlockSpec((c, HEAD_DIM_B), lambda bi, ci: (ci, 0)),
            pl.BlockSpec((c, HEAD_DIM_B), lambda bi, ci: (ci, 0)),
            _resident((N_HEADS_B, c, c), const3),
            _resident((N_HEADS_B, c, HEAD_DIM_B), const3),
            _resident((N_HEADS_B, c, HEAD_DIM_B), const3),
            _resident((N_HEADS_B, 1, HEAD_DIM_B), const3),
            _resident((1, D_B), lambda bi, ci: (0, 0)),
        ],
        out_specs=pl.BlockSpec((1, c, D_B), lambda bi, ci: (bi, ci, 0)),
        out_shape=jax.ShapeDtypeStruct((b, seq, D_B), jnp.bfloat16),
        scratch_shapes=[pltpu.VMEM((N_HEADS_B, HEAD_DIM_B, HEAD_DIM_B), jnp.float32)],
        compiler_params=pltpu.CompilerParams(
            dimension_semantics=("arbitrary", "arbitrary"), vmem_limit_bytes=VMEM_LIMIT),
        name="retention",
    )(ob3, ob3, ob3, ob3, cos, sin, dec, qdec, kdec, cdec, gn)


def _ffn_kernel(x_ref, a_ref, b_ref, woa_ref, wob_ref, gffn_ref, wg_ref, wu_ref, wd_ref, gfin_ref,
                out_ref):
    x1 = (x_ref[...]
          + jnp.dot(a_ref[...], woa_ref[...], preferred_element_type=jnp.float32)
          + jnp.dot(b_ref[...], wob_ref[...], preferred_element_type=jnp.float32))
    h2 = (x1 * lax.rsqrt(jnp.mean(x1 * x1, axis=-1, keepdims=True) + EPS) * gffn_ref[...]).astype(jnp.bfloat16)
    gate = jnp.dot(h2, wg_ref[...], preferred_element_type=jnp.float32)
    up = jnp.dot(h2, wu_ref[...], preferred_element_type=jnp.float32)
    u = (gate * jax.nn.sigmoid(gate) * up).astype(jnp.bfloat16)
    x2 = x1 + jnp.dot(u, wd_ref[...], preferred_element_type=jnp.float32)
    out_ref[...] = x2 * lax.rsqrt(jnp.mean(x2 * x2, axis=-1, keepdims=True) + EPS) * gfin_ref[...]


def _out_ffn(x2d, a2d, b2d, woa, wob, gffn, wg, wu, wd, gfin):
    rows, d_model = x2d.shape
    d_ff = wg.shape[1]
    tm = FFN_ROWS
    const = lambda i: (0, 0)
    return pl.pallas_call(
        _ffn_kernel,
        grid=(rows // tm,),
        in_specs=[
            pl.BlockSpec((tm, d_model), lambda i: (i, 0)),
            pl.BlockSpec((tm, D_A), lambda i: (i, 0)),
            pl.BlockSpec((tm, D_B), lambda i: (i, 0)),
            _resident((D_A, d_model), const),
            _resident((D_B, d_model), const),
            _resident((1, d_model), const),
            _resident((d_model, d_ff), const),
            _resident((d_model, d_ff), const),
            _resident((d_ff, d_model), const),
            _resident((1, d_model), const),
        ],
        out_specs=pl.BlockSpec((tm, d_model), lambda i: (i, 0)),
        out_shape=jax.ShapeDtypeStruct((rows, d_model), jnp.float32),
        compiler_params=pltpu.CompilerParams(
            dimension_semantics=("arbitrary",), vmem_limit_bytes=VMEM_LIMIT),
        name="out_proj_swiglu",
    )(x2d, a2d, b2d, woa, wob, gffn, wg, wu, wd, gfin)


def kernel(x, norm_mix_g, w_in, idx_k_ln_g, idx_k_ln_b, rel_bias, ret_gn_g, w_out, norm_ffn_g,
           w_gate, w_up, w_down, norm_final_g):
    b, seq, d_model = x.shape
    assert seq % ATT_TQ == 0 and seq % RET_CHUNK == 0 and (b * seq) % PROJ_ROWS == 0
    assert norm_mix_g.shape[0] == 1, "single-layer block"
    bf = jnp.bfloat16
    rows = b * seq
    n_c = seq // ATT_TK
    o = 0
    w = w_in[0]
    qa_w = w[:, o:o + D_A] * (HEAD_DIM_A ** -0.5 * LOG2E); o += D_A
    ka_w = w[:, o:o + D_A]; o += D_A
    va_w = w[:, o:o + D_A]; o += D_A
    qi_w = w[:, o:o + N_IDX_HEADS * IDX_DIM] * (IDX_DIM ** -0.5); o += N_IDX_HEADS * IDX_DIM
    small_w = w[:, o:o + IDX_DIM + N_IDX_HEADS]; o += IDX_DIM + N_IDX_HEADS
    b_w = w[:, o:]
    wa = jnp.concatenate([qa_w, ka_w, va_w, qi_w], axis=1).astype(bf)
    ws = jnp.pad(small_w, ((0, 0), (0, LANES - small_w.shape[1]))).astype(bf)
    wb = b_w.astype(bf)
    lng = jnp.pad(idx_k_ln_g[0], (0, LANES - IDX_DIM)).reshape(1, LANES)
    lnb = jnp.pad(idx_k_ln_b[0], (0, LANES - IDX_DIM)).reshape(1, LANES)

    x2d = x.reshape(rows, d_model)
    qat, qit, ka, vt, kidx, wt, ob = _project(x2d, norm_mix_g[0].reshape(1, d_model), wa, ws, wb, lng, lnb)

    n_q, n_pairs = seq // ATT_TQ, N_HEADS_A // 2
    out_a = _sparse_attention(
        kidx.reshape(b, n_c, ATT_TK, LANES), ka.reshape(b, n_c, ATT_TK, D_A),
        vt.reshape(b, n_c, N_HEADS_A * V_ROWS, ATT_TK), qit.reshape(b, n_q, n_pairs, LANES, 2 * ATT_TQ),
        qat.reshape(b, n_q, n_pairs, LANES, 2 * ATT_TQ), wt, _bias_tiles(rel_bias)).reshape(rows, D_A)

    out_b = _retention(ob.reshape(b, seq, 4 * D_B), ret_gn_g[0].reshape(1, D_B)).reshape(rows, D_B)

    wo = w_out[0]
    out = _out_ffn(x2d, out_a, out_b, wo[:D_A].astype(bf), wo[D_A:].astype(bf),
                   norm_ffn_g[0].reshape(1, d_model), w_gate[0].astype(bf), w_up[0].astype(bf),
                   w_down[0].astype(bf), norm_final_g.reshape(1, d_model))
    return out.reshape(b, seq, d_model)
```

```python
import math

import jax
import jax.numpy as jnp
import numpy as np
from jax import lax
from jax.experimental import pallas as pl
from jax.experimental.pallas import tpu as pltpu

N_HEADS_A = 8
HEAD_DIM_A = 64
N_IDX_HEADS = 8
IDX_DIM = 64
TOPK_MAX = 256
N_HEADS_B = 4
HEAD_DIM_B = 128
RET_THETA_BASE = 10000.0
N_BUCKETS = 32
MAX_DISTANCE = 128
EPS = 1e-6

D_A = N_HEADS_A * HEAD_DIM_A
D_B = N_HEADS_B * HEAD_DIM_B
BF16_ROWS = 16
V_ROWS = HEAD_DIM_A + BF16_ROWS
LOG2E = math.log2(math.e)

LANES = 128
MXU_COLS = 256

PROJ_ROWS = 512
ATT_TQ = MXU_COLS
ATT_TK = LANES
COUNT_UNROLL = 8
ATT_STEPS = 4
RET_CHUNK = 256
FFN_ROWS = 512
VMEM_LIMIT = 56 * 1024 * 1024

NEG_BIG = -1e30
INT_MIN = -(2 ** 31)
assert ATT_TQ == 2 * ATT_TK


def _resident(block_shape, index_map):
    return pl.BlockSpec(block_shape, index_map, pipeline_mode=pl.Buffered(1))


def _proj_kernel(x_ref, g_ref, wa_ref, ws_ref, wb_ref, lng_ref, lnb_ref,
                 qat_ref, qit_ref, ka_ref, vt_ref, okidx_ref, owt_ref, ob_ref):
    bf = jnp.bfloat16
    x = x_ref[...]
    tm = x.shape[0]
    h = x * lax.rsqrt(jnp.mean(x * x, axis=-1, keepdims=True) + EPS) * g_ref[...]
    hb = h.astype(bf)

    def part(k):
        return jnp.dot(hb, wa_ref[:, k * D_A:(k + 1) * D_A], preferred_element_type=jnp.float32)

    ka_ref[...] = part(1).astype(bf)
    va = part(2)
    row = lax.broadcasted_iota(jnp.int32, (V_ROWS - HEAD_DIM_A, ATT_TK), 0)
    ones_rows = jnp.where(row == 0, 1.0, 0.0).astype(bf)
    for r in range(tm // ATT_TK):
        v_t = va[r * ATT_TK:(r + 1) * ATT_TK, :].T.astype(bf)
        for hh in range(N_HEADS_A):
            vt_ref[r, hh * V_ROWS:hh * V_ROWS + HEAD_DIM_A, :] = v_t[hh * HEAD_DIM_A:(hh + 1) * HEAD_DIM_A]
            vt_ref[r, hh * V_ROWS + HEAD_DIM_A:(hh + 1) * V_ROWS, :] = ones_rows
    qa = part(0)
    qi = part(3)
    hd, tq = HEAD_DIM_A, ATT_TQ
    zero_q = jnp.zeros((hd, tq), bf)
    for qb in range(tm // tq):
        qat = qa[qb * tq:(qb + 1) * tq, :].T.astype(bf)
        qit = qi[qb * tq:(qb + 1) * tq, :].T.astype(bf)
        for j in range(N_HEADS_A // 2):
            even = slice(2 * j * hd, (2 * j + 1) * hd)
            odd = slice((2 * j + 1) * hd, (2 * j + 2) * hd)
            qat_ref[qb, j, 0:hd, 0:tq] = qat[even]
            qat_ref[qb, j, hd:2 * hd, 0:tq] = zero_q
            qat_ref[qb, j, 0:hd, tq:2 * tq] = zero_q
            qat_ref[qb, j, hd:2 * hd, tq:2 * tq] = qat[odd]
            qit_ref[qb, j, 0:hd, 0:tq] = qit[even]
            qit_ref[qb, j, 0:hd, tq:2 * tq] = qit[odd]
            qit_ref[qb, j, hd:2 * hd, 0:tq] = zero_q
            qit_ref[qb, j, hd:2 * hd, tq:2 * tq] = zero_q
    ob_ref[...] = jnp.dot(hb, wb_ref[...], preferred_element_type=jnp.float32)
    s = jnp.dot(hb, ws_ref[...], preferred_element_type=jnp.float32)
    lane = lax.broadcasted_iota(jnp.int32, s.shape, 1)
    is_k = lane < IDX_DIM
    mu = jnp.sum(jnp.where(is_k, s, 0.0), axis=-1, keepdims=True) * (1.0 / IDX_DIM)
    d = jnp.where(is_k, s - mu, 0.0)
    var = jnp.sum(d * d, axis=-1, keepdims=True) * (1.0 / IDX_DIM)
    kn = d * lax.rsqrt(var + EPS) * lng_ref[...] + lnb_ref[...]
    okidx_ref[...] = jnp.where(is_k, kn, 0.0).astype(okidx_ref.dtype)
    owt_ref[...] = (s * (N_IDX_HEADS ** -0.5)).T[IDX_DIM:IDX_DIM + N_IDX_HEADS, :]


def _project(x2, g, wa, ws, wb, lng, lnb):
    rows, d_model = x2.shape
    tm = PROJ_ROWS
    na, nb = wa.shape[1], wb.shape[1]
    n_pairs = N_HEADS_A // 2
    const = lambda i: (0, 0)
    return pl.pallas_call(
        _proj_kernel,
        grid=(rows // tm,),
        in_specs=[
            pl.BlockSpec((tm, d_model), lambda i: (i, 0)),
            _resident((1, d_model), const),
            _resident((d_model, na), const),
            _resident((d_model, LANES), const),
            _resident((d_model, nb), const),
            _resident((1, LANES), const),
            _resident((1, LANES), const),
        ],
        out_specs=[
            pl.BlockSpec((tm // ATT_TQ, n_pairs, LANES, 2 * ATT_TQ), lambda i: (i, 0, 0, 0)),
            pl.BlockSpec((tm // ATT_TQ, n_pairs, LANES, 2 * ATT_TQ), lambda i: (i, 0, 0, 0)),
            pl.BlockSpec((tm, D_A), lambda i: (i, 0)),
            pl.BlockSpec((tm // ATT_TK, N_HEADS_A * V_ROWS, ATT_TK), lambda i: (i, 0, 0)),
            pl.BlockSpec((tm, LANES), lambda i: (i, 0)),
            pl.BlockSpec((N_IDX_HEADS, tm), lambda i: (0, i)),
            pl.BlockSpec((tm, nb), lambda i: (i, 0)),
        ],
        out_shape=[
            jax.ShapeDtypeStruct((rows // ATT_TQ, n_pairs, LANES, 2 * ATT_TQ), jnp.bfloat16),
            jax.ShapeDtypeStruct((rows // ATT_TQ, n_pairs, LANES, 2 * ATT_TQ), jnp.bfloat16),
            jax.ShapeDtypeStruct((rows, D_A), jnp.bfloat16),
            jax.ShapeDtypeStruct((rows // ATT_TK, N_HEADS_A * V_ROWS, ATT_TK), jnp.bfloat16),
            jax.ShapeDtypeStruct((rows, LANES), jnp.bfloat16),
            jax.ShapeDtypeStruct((N_IDX_HEADS, rows), jnp.float32),
            jax.ShapeDtypeStruct((rows, nb), jnp.float32),
        ],
        compiler_params=pltpu.CompilerParams(
            dimension_semantics=("arbitrary",), vmem_limit_bytes=VMEM_LIMIT),
        name="rmsnorm_in_proj",
    )(x2, g, wa, ws, wb, lng, lnb)


def _attn_kernel(kidx_ref, ka_ref, vt_ref, qit_ref, qat_ref, wt_ref, bias_ref, out_ref,
                 keys_ref, half_ref, lg_ref, p_ref, alpha_ref, m_ref, acc_ref, outt_ref, *, index_bits):
    i = pl.program_id(1)
    tq, tk = ATT_TQ, ATT_TK
    n_pairs = N_HEADS_A // 2
    n_chunks = (i + 1) * (tq // tk)
    dummy_chunk = keys_ref.shape[0] - 1
    t_idx = i * tq + lax.broadcasted_iota(jnp.int32, (1, tq), 1)

    lg_ref[...] = jnp.zeros(lg_ref.shape, jnp.float32)

    def idx_dots(c, buf):
        kc = kidx_ref[0, c]
        for j in range(N_IDX_HEADS // 2):
            d = jnp.dot(kc, qit_ref[0, 0, j], preferred_element_type=jnp.float32)
            lg_ref[buf, 2 * j] = d[:, :tq]
            lg_ref[buf, 2 * j + 1] = d[:, tq:]

    def idx_reduce(c, valid, buf):
        cw = jnp.where(valid, c, dummy_chunk)
        for hf in range(tq // LANES):
            ls = slice(hf * LANES, (hf + 1) * LANES)
            acc = jnp.zeros((tk, LANES), jnp.float32)
            for h in range(N_IDX_HEADS):
                acc = acc + wt_ref[h:h + 1, ls] * jnp.maximum(lg_ref[buf, h, :, ls], 0.0)
            bits = lax.bitcast_convert_type(acc + 0.0, jnp.int32)
            key = jnp.where(bits < 0, bits ^ jnp.int32(0x7FFFFFFF), bits)
            s_idx = c * tk + lax.broadcasted_iota(jnp.int32, (tk, LANES), 0)
            key = jnp.where(s_idx <= t_idx[:, ls], key, jnp.int32(INT_MIN))
            keys_ref[cw, :, ls] = key
            half_ref[cw, :, ls] = lax.shift_right_arithmetic(key, 16).astype(jnp.int16)

    def idx_step(s, buf):
        idx_dots(jnp.minimum(s, n_chunks - 1), buf)
        idx_reduce(jnp.clip(s - 1, 0, n_chunks - 1), (s >= 1) & (s <= n_chunks), 1 - buf)

    def idx_steps(u, carry):
        for k in range(ATT_STEPS):
            idx_step(ATT_STEPS * u + k, k % 2)
        return carry

    lax.fori_loop(0, (n_chunks + 1 + ATT_STEPS - 1) // ATT_STEPS, idx_steps, 0)

    k_row = jnp.minimum(t_idx + 1, TOPK_MAX)
    I16_MIN = -(2 ** 15)

    def count(pred):
        def body(c, acc):
            return acc + jnp.where(pred(c, keys_ref[c]), 1, 0).astype(jnp.int32)
        acc = lax.fori_loop(0, n_chunks, body, jnp.zeros((tk, tq), jnp.int32))
        return jnp.sum(acc, axis=0, keepdims=True)

    one16 = jnp.ones((tk, tq), jnp.int16)
    zero16 = jnp.zeros((tk, tq), jnp.int16)

    def count16(pred):
        def chunks(first, n):
            def body(g, acc):
                for u in range(n):
                    acc = jnp.where(pred(half_ref[first + n * g + u]), acc + one16, acc)
                return acc
            return body
        n_long = n_chunks // COUNT_UNROLL
        acc = lax.fori_loop(0, n_long, chunks(0, COUNT_UNROLL), jnp.zeros((tk, tq), jnp.int16))
        acc = lax.fori_loop(0, (n_chunks - COUNT_UNROLL * n_long) // 2,
                            chunks(COUNT_UNROLL * n_long, 2), acc)
        part = acc[0:BF16_ROWS]
        for r0 in range(BF16_ROWS, tk, BF16_ROWS):
            part = part + acc[r0:r0 + BF16_ROWS]
        return jnp.sum(part.astype(jnp.int32), axis=0, keepdims=True)

    def kth_largest16(k_need, n_init):
        def bit_round(r, st):
            prefix, n_at = st
            cand = prefix + lax.shift_left(jnp.int32(1), 15 - r)
            cand16 = jnp.broadcast_to(cand, (tk, tq)).astype(jnp.int16)
            n_cand = count16(lambda v: v >= cand16)
            ok = n_cand >= k_need
            return jnp.where(ok, cand, prefix), jnp.where(ok, n_cand, n_at)

        return lax.fori_loop(0, 16, bit_round, (jnp.full((1, tq), I16_MIN, jnp.int32), n_init))

    n_all = jnp.broadcast_to(n_chunks * tk, (1, tq)).astype(jnp.int32)
    v_hi, n_hi_ge = kth_largest16(k_row, n_all)
    v_hi16 = jnp.broadcast_to(v_hi, (tk, tq)).astype(jnp.int16)
    n_hi_gt = count16(lambda v: v > v_hi16)
    k_low = k_row - n_hi_gt

    def low_halves(c, carry):
        key = keys_ref[c]
        low = (key & jnp.int32(0xFFFF)) + I16_MIN
        in_bucket = lax.shift_right_arithmetic(key, 16) == v_hi
        half_ref[c] = jnp.where(in_bucket, low, I16_MIN).astype(jnp.int16)
        return carry
    lax.fori_loop(0, n_chunks, low_halves, 0)

    v_lo, n_lo_ge = kth_largest16(k_low, n_hi_ge - n_hi_gt)
    vstar = lax.shift_left(v_hi, 16) + (v_lo - I16_MIN)

    n_ge = n_hi_gt + n_lo_ge

    @pl.when(jnp.max(n_ge - k_row) > 0)
    def _():
        need = k_row - count(lambda c, k: k > vstar)

        def idx_round(r, x):
            cand = x + lax.shift_left(jnp.int32(1), index_bits - 1 - r)

            def tie_below(c, k):
                s_idx = c * tk + lax.broadcasted_iota(jnp.int32, (tk, tq), 0)
                return (k == vstar) & (s_idx < cand)
            return jnp.where(count(tie_below) < need, cand, x)

        last = lax.fori_loop(0, index_bits, idx_round, jnp.zeros((1, tq), jnp.int32))

        def drop(c, carry):
            k = keys_ref[c]
            s_idx = c * tk + lax.broadcasted_iota(jnp.int32, (tk, tq), 0)
            keys_ref[c] = jnp.where((k == vstar) & (s_idx > last), vstar - 1, k)
            return carry
        lax.fori_loop(0, n_chunks, drop, 0)

    def to_mask(c, carry):
        maskb = jnp.where(keys_ref[c] >= vstar, 0.0, NEG_BIG)
        keys_ref[c] = lax.bitcast_convert_type(maskb, jnp.int32)
        return carry
    lax.fori_loop(0, n_chunks, to_mask, 0)
    keys_ref[dummy_chunk] = lax.bitcast_convert_type(jnp.full((tk, tq), NEG_BIG, jnp.float32), jnp.int32)

    m_ref[...] = jnp.full(m_ref.shape, NEG_BIG, jnp.float32)
    acc_ref[...] = jnp.zeros(acc_ref.shape, jnp.float32)
    p_ref[...] = jnp.zeros(p_ref.shape, p_ref.dtype)
    alpha_ref[...] = jnp.ones(alpha_ref.shape, jnp.float32)
    first_near = (tq // tk) * i - 1

    def pv_stage(c, buf):
        for h in range(N_HEADS_A):
            rs = slice(V_ROWS * h, V_ROWS * (h + 1))
            pv = jnp.dot(vt_ref[0, c, rs, :], p_ref[buf, h], preferred_element_type=jnp.float32)
            acc_ref[rs, :] = alpha_ref[buf, h:h + 1, :] * acc_ref[rs, :] + pv

    def qk_stage(c, buf):
        for j in range(n_pairs):
            d = jnp.dot(ka_ref[0, c, :, LANES * j:LANES * (j + 1)], qat_ref[0, 0, j],
                        preferred_element_type=jnp.float32)
            lg_ref[buf, 2 * j] = d[:, :tq]
            lg_ref[buf, 2 * j + 1] = d[:, tq:]

    def softmax_stage(c, valid, buf, near):
        cm = jnp.where(valid, c, dummy_chunk)
        tbl = jnp.where(valid, jnp.clip(c - first_near + 1, 0, 3), 0)
        for h in range(N_HEADS_A):
            for hf in range(tq // LANES):
                ls = slice(hf * LANES, (hf + 1) * LANES)
                lm = lg_ref[buf, h, :, ls] + lax.bitcast_convert_type(keys_ref[cm, :, ls], jnp.float32)
                if near:
                    lm = lm + bias_ref[tbl, h, :, ls]
                m_old = m_ref[h:h + 1, ls]
                m_new = jnp.maximum(m_old, jnp.max(lm, axis=0, keepdims=True))
                alpha = jnp.exp2(m_old - m_new)
                p = jnp.exp2(lm - m_new)
                m_ref[h:h + 1, ls] = m_new
                alpha_ref[buf, h:h + 1, ls] = alpha
                p_ref[buf, h, :, ls] = p.astype(p_ref.dtype)

    def step(s, buf, near):
        qk_stage(jnp.minimum(s, n_chunks - 1), buf)
        pv_stage(jnp.clip(s - 2, 0, n_chunks - 1), buf)
        sm = s - 1
        softmax_stage(jnp.clip(sm, 0, n_chunks - 1), (sm >= 0) & (sm < n_chunks), 1 - buf, near)

    def steps(near):
        def body(u, carry):
            for k in range(ATT_STEPS):
                step(ATT_STEPS * u + k, k % 2, near)
            return carry
        return body

    n_bodies = (n_chunks + 2 + ATT_STEPS - 1) // ATT_STEPS
    n_far = jnp.maximum(first_near + 1, 0) // ATT_STEPS
    lax.fori_loop(0, n_far, steps(False), 0)
    lax.fori_loop(n_far, n_bodies, steps(True), 0)

    for h in range(N_HEADS_A):
        r0 = V_ROWS * h
        inv = 1.0 / acc_ref[r0 + HEAD_DIM_A:r0 + HEAD_DIM_A + 1, :]
        outt_ref[HEAD_DIM_A * h:HEAD_DIM_A * (h + 1), :] = acc_ref[r0:r0 + HEAD_DIM_A, :] * inv
    out_ref[0] = outt_ref[...].T.astype(out_ref.dtype)


def _bias_tiles(rel_bias):
    tq, tk = ATT_TQ, ATT_TK
    lo = -(tq - 1)
    dist = np.arange(lo, tk + tq)
    n = np.maximum(dist, 0)
    max_exact = N_BUCKETS // 2
    nf = np.maximum(n, max_exact).astype(np.float64)
    large = max_exact + (np.log(nf / max_exact) / math.log(MAX_DISTANCE / max_exact)
                         * (N_BUCKETS - max_exact)).astype(np.int32)
    bucket = np.where(n < max_exact, n, np.minimum(large, N_BUCKETS - 1)).astype(np.int32)
    vec = ((rel_bias[bucket] - rel_bias[N_BUCKETS - 1][None, :]) * LOG2E).T
    tiles = [jnp.zeros((N_HEADS_A, tk, tq), rel_bias.dtype)]
    for base in (tk, 0, -tk):
        rows = [lax.slice_in_dim(vec, base - r - lo, base - r - lo + tq, axis=1) for r in range(tk)]
        tiles.append(jnp.stack(rows, axis=1))
    return jnp.stack(tiles)


def _sparse_attention(kidx4, ka4, vt4, qit2, qat2, wt, bias_tiles):
    b, n_c = kidx4.shape[0], kidx4.shape[1]
    seq = n_c * ATT_TK
    tq = ATT_TQ
    n_pairs = N_HEADS_A // 2
    kern = lambda *refs: _attn_kernel(*refs, index_bits=max(1, (seq - 1).bit_length()))
    return pl.pallas_call(
        kern,
        grid=(b, seq // tq),
        in_specs=[
            _resident((1, n_c, ATT_TK, LANES), lambda bi, i: (bi, 0, 0, 0)),
            _resident((1, n_c, ATT_TK, D_A), lambda bi, i: (bi, 0, 0, 0)),
            _resident((1, n_c, N_HEADS_A * V_ROWS, ATT_TK), lambda bi, i: (bi, 0, 0, 0)),
            pl.BlockSpec((1, 1, N_IDX_HEADS // 2, LANES, 2 * tq), lambda bi, i: (bi, i, 0, 0, 0)),
            pl.BlockSpec((1, 1, n_pairs, LANES, 2 * tq), lambda bi, i: (bi, i, 0, 0, 0)),
            pl.BlockSpec((N_IDX_HEADS, tq), lambda bi, i: (0, bi * (seq // tq) + i)),
            _resident((4, N_HEADS_A, ATT_TK, tq), lambda bi, i: (0, 0, 0, 0)),
        ],
        out_specs=pl.BlockSpec((1, tq, D_A), lambda bi, i: (bi, i, 0)),
        out_shape=jax.ShapeDtypeStruct((b, seq, D_A), jnp.bfloat16),
        scratch_shapes=[
            pltpu.VMEM((n_c + 1, ATT_TK, tq), jnp.int32),
            pltpu.VMEM((n_c + 1, ATT_TK, tq), jnp.int16),
            pltpu.VMEM((2, N_HEADS_A, ATT_TK, tq), jnp.float32),
            pltpu.VMEM((2, N_HEADS_A, ATT_TK, tq), jnp.bfloat16),
            pltpu.VMEM((2, N_HEADS_A, tq), jnp.float32),
            pltpu.VMEM((N_HEADS_A, tq), jnp.float32),
            pltpu.VMEM((N_HEADS_A * V_ROWS, tq), jnp.float32),
            pltpu.VMEM((D_A, tq), jnp.float32),
        ],
        compiler_params=pltpu.CompilerParams(
            dimension_semantics=("arbitrary", "arbitrary"), vmem_limit_bytes=VMEM_LIMIT),
        name="indexer_topk_attention",
    )(kidx4, ka4, vt4, qit2, qat2, wt, bias_tiles)


def _ret_kernel(q_ref, k_ref, v_ref, g_ref, cos_ref, sin_ref, dec_ref, qdec_ref, kdec_ref,
                cdec_ref, gn_ref, out_ref, state_ref):
    @pl.when(pl.program_id(1) == 0)
    def _():
        state_ref[...] = jnp.zeros(state_ref.shape, jnp.float32)

    cos = cos_ref[...]
    sin = sin_ref[...]
    half = HEAD_DIM_B // 2
    for h in range(N_HEADS_B):
        lo = HEAD_DIM_B * h
        q = q_ref[0, :, lo:lo + HEAD_DIM_B]
        k = k_ref[0, :, lo:lo + HEAD_DIM_B]
        qr = q * cos + pltpu.roll(q, half, 1) * sin
        ks = (k * cos + pltpu.roll(k, half, 1) * sin) * (HEAD_DIM_B ** -0.5)
        qb = qr.astype(jnp.bfloat16)
        vb = v_ref[0, :, lo:lo + HEAD_DIM_B].astype(jnp.bfloat16)
        qk = lax.dot_general(qb, ks.astype(jnp.bfloat16), (((1,), (1,)), ((), ())),
                             preferred_element_type=jnp.float32)
        inner = (qk * dec_ref[h]).astype(jnp.bfloat16)
        state = state_ref[h]
        y = (jnp.dot(inner, vb, preferred_element_type=jnp.float32)
             + jnp.dot(qb, state.astype(jnp.bfloat16), preferred_element_type=jnp.float32) * qdec_ref[h])
        kd_t = (ks * kdec_ref[h]).T.astype(jnp.bfloat16)
        state_ref[h] = state * cdec_ref[h] + jnp.dot(kd_t, vb, preferred_element_type=jnp.float32)
        yn = y * lax.rsqrt(jnp.mean(y * y, axis=-1, keepdims=True) + EPS) * gn_ref[:, lo:lo + HEAD_DIM_B]
        g = g_ref[0, :, lo:lo + HEAD_DIM_B]
        out_ref[0, :, lo:lo + HEAD_DIM_B] = (yn * (g * jax.nn.sigmoid(g))).astype(out_ref.dtype)


def _retention_tables(seq):
    c = RET_CHUNK
    half = HEAD_DIM_B // 2
    theta = 1.0 / (RET_THETA_BASE ** jnp.linspace(0.0, 1.0, half, dtype=jnp.float32))
    ang = jnp.arange(seq, dtype=jnp.int32).astype(jnp.float32)[:, None] * theta[None, :]
    cos = jnp.concatenate([jnp.cos(ang), jnp.cos(ang)], axis=-1)
    sin = jnp.concatenate([-jnp.sin(ang), jnp.sin(ang)], axis=-1)
    gamma = 1.0 - 2.0 ** (-5.0 - jnp.arange(N_HEADS_B, dtype=jnp.float32))
    log_g = jnp.log(gamma)
    pos = jnp.arange(c, dtype=jnp.float32)
    diff = pos[:, None] - pos[None, :]
    dec = jnp.where(diff >= 0, jnp.exp(log_g[:, None, None] * jnp.maximum(diff, 0.0)), 0.0)
    qdec = jnp.exp(log_g[:, None] * (pos[None, :] + 1.0))
    kdec = jnp.exp(log_g[:, None] * (c - 1.0 - pos[None, :]))
    cdec = jnp.exp(log_g * c)
    bc = lambda a: jnp.broadcast_to(a[:, :, None], (N_HEADS_B, c, HEAD_DIM_B))
    cdec_b = jnp.broadcast_to(cdec[:, None, None], (N_HEADS_B, 1, HEAD_DIM_B))
    return cos, sin, dec, bc(qdec), bc(kdec), cdec_b


def _retention(ob3, gn):
    b, seq, _ = ob3.shape
    c = RET_CHUNK
    cos, sin, dec, qdec, kdec, cdec = _retention_tables(seq)
    col = lambda j: (lambda bi, ci: (bi, ci, j))
    const3 = lambda bi, ci: (0, 0, 0)
    return pl.pallas_call(
        _ret_kernel,
        grid=(b, seq // c),
        in_specs=[
            pl.BlockSpec((1, c, D_B), col(0)),
            pl.BlockSpec((1, c, D_B), col(1)),
            pl.BlockSpec((1, c, D_B), col(2)),
            pl.BlockSpec((1, c, D_B), col(3)),
            pl.BlockSpec((c, HEAD_DIM_B), lambda bi, ci: (ci, 0)),
            pl.BlockSpec((c, HEAD_DIM_B), lambda bi, ci: (ci, 0)),
            _resident((N_HEADS_B, c, c), const3),
            _resident((N_HEADS_B, c, HEAD_DIM_B), const3),
            _resident((N_HEADS_B, c, HEAD_DIM_B), const3),
            _resident((N_HEADS_B, 1, HEAD_DIM_B), const3),
            _resident((1, D_B), lambda bi, ci: (0, 0)),
        ],
        out_specs=pl.BlockSpec((1, c, D_B), lambda bi, ci: (bi, ci, 0)),
        out_shape=jax.ShapeDtypeStruct((b, seq, D_B), jnp.bfloat16),
        scratch_shapes=[pltpu.VMEM((N_HEADS_B, HEAD_DIM_B, HEAD_DIM_B), jnp.float32)],
        compiler_params=pltpu.CompilerParams(
            dimension_semantics=("arbitrary", "arbitrary"), vmem_limit_bytes=VMEM_LIMIT),
        name="retention",
    )(ob3, ob3, ob3, ob3, cos, sin, dec, qdec, kdec, cdec, gn)


def _ffn_kernel(x_ref, a_ref, b_ref, woa_ref, wob_ref, gffn_ref, wg_ref, wu_ref, wd_ref, gfin_ref,
                out_ref):
    x1 = (x_ref[...]
          + jnp.dot(a_ref[...], woa_ref[...], preferred_element_type=jnp.float32)
          + jnp.dot(b_ref[...], wob_ref[...], preferred_element_type=jnp.float32))
    h2 = (x1 * lax.rsqrt(jnp.mean(x1 * x1, axis=-1, keepdims=True) + EPS) * gffn_ref[...]).astype(jnp.bfloat16)
    gate = jnp.dot(h2, wg_ref[...], preferred_element_type=jnp.float32)
    up = jnp.dot(h2, wu_ref[...], preferred_element_type=jnp.float32)
    u = (gate * jax.nn.sigmoid(gate) * up).astype(jnp.bfloat16)
    x2 = x1 + jnp.dot(u, wd_ref[...], preferred_element_type=jnp.float32)
    out_ref[...] = x2 * lax.rsqrt(jnp.mean(x2 * x2, axis=-1, keepdims=True) + EPS) * gfin_ref[...]


def _out_ffn(x2d, a2d, b2d, woa, wob, gffn, wg, wu, wd, gfin):
    rows, d_model = x2d.shape
    d_ff = wg.shape[1]
    tm = FFN_ROWS
    const = lambda i: (0, 0)
    return pl.pallas_call(
        _ffn_kernel,
        grid=(rows // tm,),
        in_specs=[
            pl.BlockSpec((tm, d_model), lambda i: (i, 0)),
            pl.BlockSpec((tm, D_A), lambda i: (i, 0)),
            pl.BlockSpec((tm, D_B), lambda i: (i, 0)),
            _resident((D_A, d_model), const),
            _resident((D_B, d_model), const),
            _resident((1, d_model), const),
            _resident((d_model, d_ff), const),
            _resident((d_model, d_ff), const),
            _resident((d_ff, d_model), const),
            _resident((1, d_model), const),
        ],
        out_specs=pl.BlockSpec((tm, d_model), lambda i: (i, 0)),
        out_shape=jax.ShapeDtypeStruct((rows, d_model), jnp.float32),
        compiler_params=pltpu.CompilerParams(
            dimension_semantics=("arbitrary",), vmem_limit_bytes=VMEM_LIMIT),
        name="out_proj_swiglu",
    )(x2d, a2d, b2d, woa, wob, gffn, wg, wu, wd, gfin)


def kernel(x, norm_mix_g, w_in, idx_k_ln_g, idx_k_ln_b, rel_bias, ret_gn_g, w_out, norm_ffn_g,
           w_gate, w_up, w_down, norm_final_g):
    b, seq, d_model = x.shape
    assert seq % ATT_TQ == 0 and seq % RET_CHUNK == 0 and (b * seq) % PROJ_ROWS == 0
    assert norm_mix_g.shape[0] == 1, "single-layer block"
    bf = jnp.bfloat16
    rows = b * seq
    n_c = seq // ATT_TK
    o = 0
    w = w_in[0]
    qa_w = w[:, o:o + D_A] * (HEAD_DIM_A ** -0.5 * LOG2E); o += D_A
    ka_w = w[:, o:o + D_A]; o += D_A
    va_w = w[:, o:o + D_A]; o += D_A
    qi_w = w[:, o:o + N_IDX_HEADS * IDX_DIM] * (IDX_DIM ** -0.5); o += N_IDX_HEADS * IDX_DIM
    small_w = w[:, o:o + IDX_DIM + N_IDX_HEADS]; o += IDX_DIM + N_IDX_HEADS
    b_w = w[:, o:]
    wa = jnp.concatenate([qa_w, ka_w, va_w, qi_w], axis=1).astype(bf)
    ws = jnp.pad(small_w, ((0, 0), (0, LANES - small_w.shape[1]))).astype(bf)
    wb = b_w.astype(bf)
    lng = jnp.pad(idx_k_ln_g[0], (0, LANES - IDX_DIM)).reshape(1, LANES)
    lnb = jnp.pad(idx_k_ln_b[0], (0, LANES - IDX_DIM)).reshape(1, LANES)

    x2d = x.reshape(rows, d_model)
    qat, qit, ka, vt, kidx, wt, ob = _project(x2d, norm_mix_g[0].reshape(1, d_model), wa, ws, wb, lng, lnb)

    n_q, n_pairs = seq // ATT_TQ, N_HEADS_A // 2
    out_a = _sparse_attention(
        kidx.reshape(b, n_c, ATT_TK, LANES), ka.reshape(b, n_c, ATT_TK, D_A),
        vt.reshape(b, n_c, N_HEADS_A * V_ROWS, ATT_TK), qit.reshape(b, n_q, n_pairs, LANES, 2 * ATT_TQ),
        qat.reshape(b, n_q, n_pairs, LANES, 2 * ATT_TQ), wt, _bias_tiles(rel_bias)).reshape(rows, D_A)

    out_b = _retention(ob.reshape(b, seq, 4 * D_B), ret_gn_g[0].reshape(1, D_B)).reshape(rows, D_B)

    wo = w_out[0]
    out = _out_ffn(x2d, out_a, out_b, wo[:D_A].astype(bf), wo[D_A:].astype(bf),
                   norm_ffn_g[0].reshape(1, d_model), w_gate[0].astype(bf), w_up[0].astype(bf),
                   w_down[0].astype(bf), norm_final_g.reshape(1, d_model))
    return out.reshape(b, seq, d_model)
```

```python
import math

import jax
import jax.numpy as jnp
import numpy as np
from jax import lax
from jax.experimental import pallas as pl
from jax.experimental.pallas import tpu as pltpu

N_HEADS_A = 8
HEAD_DIM_A = 64
N_IDX_HEADS = 8
IDX_DIM = 64
TOPK_MAX = 256
N_HEADS_B = 4
HEAD_DIM_B = 128
RET_THETA_BASE = 10000.0
N_BUCKETS = 32
MAX_DISTANCE = 128
EPS = 1e-6

D_A = N_HEADS_A * HEAD_DIM_A
D_B = N_HEADS_B * HEAD_DIM_B
BF16_ROWS = 16
V_ROWS = HEAD_DIM_A + BF16_ROWS
LOG2E = math.log2(math.e)

LANES = 128
MXU_COLS = 256

PROJ_ROWS = 512
ATT_TQ = MXU_COLS
ATT_TK = LANES
COUNT_UNROLL = 8
ATT_TKP = 2 * ATT_TK
ATT_STEPS = 4
ATT_PAIR_STEPS = 2
SOFTMAX_ROWS = 64
RET_CHUNK = 256
FFN_ROWS = 512
VMEM_LIMIT = 56 * 1024 * 1024

NEG_BIG = -1e30
INT_MIN = -(2 ** 31)
assert ATT_TQ == 2 * ATT_TK


def _resident(block_shape, index_map):
    return pl.BlockSpec(block_shape, index_map, pipeline_mode=pl.Buffered(1))


def _proj_kernel(x_ref, g_ref, wa_ref, ws_ref, wb_ref, lng_ref, lnb_ref,
                 qat_ref, qit_ref, ka_ref, vt_ref, okidx_ref, owt_ref, ob_ref):
    bf = jnp.bfloat16
    x = x_ref[...]
    tm = x.shape[0]
    h = x * lax.rsqrt(jnp.mean(x * x, axis=-1, keepdims=True) + EPS) * g_ref[...]
    hb = h.astype(bf)

    def part(k):
        return jnp.dot(hb, wa_ref[:, k * D_A:(k + 1) * D_A], preferred_element_type=jnp.float32)

    ka_ref[...] = part(1).astype(bf)
    va = part(2)
    row = lax.broadcasted_iota(jnp.int32, (V_ROWS - HEAD_DIM_A, ATT_TKP), 0)
    ones_rows = jnp.where(row == 0, 1.0, 0.0).astype(bf)
    for r in range(tm // ATT_TKP):
        v_t = va[r * ATT_TKP:(r + 1) * ATT_TKP, :].T.astype(bf)
        for hh in range(N_HEADS_A):
            vt_ref[r, hh * V_ROWS:hh * V_ROWS + HEAD_DIM_A, :] = v_t[hh * HEAD_DIM_A:(hh + 1) * HEAD_DIM_A]
            vt_ref[r, hh * V_ROWS + HEAD_DIM_A:(hh + 1) * V_ROWS, :] = ones_rows
    qa = part(0)
    qi = part(3)
    hd, tq = HEAD_DIM_A, ATT_TQ
    zero_q = jnp.zeros((hd, tq), bf)
    for qb in range(tm // tq):
        qat = qa[qb * tq:(qb + 1) * tq, :].T.astype(bf)
        qit = qi[qb * tq:(qb + 1) * tq, :].T.astype(bf)
        for j in range(N_HEADS_A // 2):
            even = slice(2 * j * hd, (2 * j + 1) * hd)
            odd = slice((2 * j + 1) * hd, (2 * j + 2) * hd)
            qat_ref[qb, j, 0:hd, 0:tq] = qat[even]
            qat_ref[qb, j, hd:2 * hd, 0:tq] = zero_q
            qat_ref[qb, j, 0:hd, tq:2 * tq] = zero_q
            qat_ref[qb, j, hd:2 * hd, tq:2 * tq] = qat[odd]
            qit_ref[qb, j, 0:hd, 0:tq] = qit[even]
            qit_ref[qb, j, 0:hd, tq:2 * tq] = qit[odd]
            qit_ref[qb, j, hd:2 * hd, 0:tq] = zero_q
            qit_ref[qb, j, hd:2 * hd, tq:2 * tq] = zero_q
    ob_ref[...] = jnp.dot(hb, wb_ref[...], preferred_element_type=jnp.float32)
    s = jnp.dot(hb, ws_ref[...], preferred_element_type=jnp.float32)
    lane = lax.broadcasted_iota(jnp.int32, s.shape, 1)
    is_k = lane < IDX_DIM
    mu = jnp.sum(jnp.where(is_k, s, 0.0), axis=-1, keepdims=True) * (1.0 / IDX_DIM)
    d = jnp.where(is_k, s - mu, 0.0)
    var = jnp.sum(d * d, axis=-1, keepdims=True) * (1.0 / IDX_DIM)
    kn = d * lax.rsqrt(var + EPS) * lng_ref[...] + lnb_ref[...]
    okidx_ref[...] = jnp.where(is_k, kn, 0.0).astype(okidx_ref.dtype)
    owt_ref[...] = (s * (N_IDX_HEADS ** -0.5)).T[IDX_DIM:IDX_DIM + N_IDX_HEADS, :]


def _project(x2, g, wa, ws, wb, lng, lnb):
    rows, d_model = x2.shape
    tm = PROJ_ROWS
    na, nb = wa.shape[1], wb.shape[1]
    n_pairs = N_HEADS_A // 2
    const = lambda i: (0, 0)
    return pl.pallas_call(
        _proj_kernel,
        grid=(rows // tm,),
        in_specs=[
            pl.BlockSpec((tm, d_model), lambda i: (i, 0)),
            _resident((1, d_model), const),
            _resident((d_model, na), const),
            _resident((d_model, LANES), const),
            _resident((d_model, nb), const),
            _resident((1, LANES), const),
            _resident((1, LANES), const),
        ],
        out_specs=[
            pl.BlockSpec((tm // ATT_TQ, n_pairs, LANES, 2 * ATT_TQ), lambda i: (i, 0, 0, 0)),
            pl.BlockSpec((tm // ATT_TQ, n_pairs, LANES, 2 * ATT_TQ), lambda i: (i, 0, 0, 0)),
            pl.BlockSpec((tm, D_A), lambda i: (i, 0)),
            pl.BlockSpec((tm // ATT_TKP, N_HEADS_A * V_ROWS, ATT_TKP), lambda i: (i, 0, 0)),
            pl.BlockSpec((tm, LANES), lambda i: (i, 0)),
            pl.BlockSpec((N_IDX_HEADS, tm), lambda i: (0, i)),
            pl.BlockSpec((tm, nb), lambda i: (i, 0)),
        ],
        out_shape=[
            jax.ShapeDtypeStruct((rows // ATT_TQ, n_pairs, LANES, 2 * ATT_TQ), jnp.bfloat16),
            jax.ShapeDtypeStruct((rows // ATT_TQ, n_pairs, LANES, 2 * ATT_TQ), jnp.bfloat16),
            jax.ShapeDtypeStruct((rows, D_A), jnp.bfloat16),
            jax.ShapeDtypeStruct((rows // ATT_TKP, N_HEADS_A * V_ROWS, ATT_TKP), jnp.bfloat16),
            jax.ShapeDtypeStruct((rows, LANES), jnp.bfloat16),
            jax.ShapeDtypeStruct((N_IDX_HEADS, rows), jnp.float32),
            jax.ShapeDtypeStruct((rows, nb), jnp.float32),
        ],
        compiler_params=pltpu.CompilerParams(
            dimension_semantics=("arbitrary",), vmem_limit_bytes=VMEM_LIMIT),
        name="rmsnorm_in_proj",
    )(x2, g, wa, ws, wb, lng, lnb)


def _attn_kernel(kidx_ref, ka_ref, vt_ref, qit_ref, qat_ref, wt_ref, bias_ref, out_ref,
                 keys_ref, half_ref, lg_ref, p_ref, alpha_ref, m_ref, acc_ref, outt_ref, *, index_bits):
    i = pl.program_id(1)
    tq, tk = ATT_TQ, ATT_TK
    n_pairs = N_HEADS_A // 2
    n_chunks = (i + 1) * (tq // tk)
    dummy_chunk = keys_ref.shape[0] - 1
    t_idx = i * tq + lax.broadcasted_iota(jnp.int32, (1, tq), 1)

    lg_ref[...] = jnp.zeros(lg_ref.shape, jnp.float32)

    def idx_dots(c, buf):
        kc = kidx_ref[0, c]
        for j in range(N_IDX_HEADS // 2):
            d = jnp.dot(kc, qit_ref[0, 0, j], preferred_element_type=jnp.float32)
            lg_ref[buf, 2 * j, 0:tk] = d[:, :tq]
            lg_ref[buf, 2 * j + 1, 0:tk] = d[:, tq:]

    def idx_reduce(c, valid, buf):
        cw = jnp.where(valid, c, dummy_chunk)
        for hf in range(tq // LANES):
            ls = slice(hf * LANES, (hf + 1) * LANES)
            acc = jnp.zeros((tk, LANES), jnp.float32)
            for h in range(N_IDX_HEADS):
                acc = acc + wt_ref[h:h + 1, ls] * jnp.maximum(lg_ref[buf, h, 0:tk, ls], 0.0)
            bits = lax.bitcast_convert_type(acc + 0.0, jnp.int32)
            key = jnp.where(bits < 0, bits ^ jnp.int32(0x7FFFFFFF), bits)
            s_idx = c * tk + lax.broadcasted_iota(jnp.int32, (tk, LANES), 0)
            key = jnp.where(s_idx <= t_idx[:, ls], key, jnp.int32(INT_MIN))
            keys_ref[cw, :, ls] = key
            half_ref[cw, :, ls] = lax.shift_right_arithmetic(key, 16).astype(jnp.int16)

    def idx_step(s, buf):
        idx_dots(jnp.minimum(s, n_chunks - 1), buf)
        idx_reduce(jnp.clip(s - 1, 0, n_chunks - 1), (s >= 1) & (s <= n_chunks), 1 - buf)

    def idx_steps(u, carry):
        for k in range(ATT_STEPS):
            idx_step(ATT_STEPS * u + k, k % 2)
        return carry

    lax.fori_loop(0, (n_chunks + 1 + ATT_STEPS - 1) // ATT_STEPS, idx_steps, 0)

    k_row = jnp.minimum(t_idx + 1, TOPK_MAX)
    I16_MIN = -(2 ** 15)

    def count(pred):
        def body(c, acc):
            return acc + jnp.where(pred(c, keys_ref[c]), 1, 0).astype(jnp.int32)
        acc = lax.fori_loop(0, n_chunks, body, jnp.zeros((tk, tq), jnp.int32))
        return jnp.sum(acc, axis=0, keepdims=True)

    one16 = jnp.ones((tk, tq), jnp.int16)
    zero16 = jnp.zeros((tk, tq), jnp.int16)

    def count16(pred):
        def chunks(first, n):
            def body(g, acc):
                for u in range(n):
                    acc = jnp.where(pred(half_ref[first + n * g + u]), acc + one16, acc)
                return acc
            return body
        n_long = n_chunks // COUNT_UNROLL
        acc = lax.fori_loop(0, n_long, chunks(0, COUNT_UNROLL), jnp.zeros((tk, tq), jnp.int16))
        acc = lax.fori_loop(0, (n_chunks - COUNT_UNROLL * n_long) // 2,
                            chunks(COUNT_UNROLL * n_long, 2), acc)
        part = acc[0:BF16_ROWS]
        for r0 in range(BF16_ROWS, tk, BF16_ROWS):
            part = part + acc[r0:r0 + BF16_ROWS]
        return jnp.sum(part.astype(jnp.int32), axis=0, keepdims=True)

    def kth_largest16(k_need, n_init):
        def bit_round(r, st):
            prefix, n_at = st
            cand = prefix + lax.shift_left(jnp.int32(1), 15 - r)
            cand16 = jnp.broadcast_to(cand, (tk, tq)).astype(jnp.int16)
            n_cand = count16(lambda v: v >= cand16)
            ok = n_cand >= k_need
            return jnp.where(ok, cand, prefix), jnp.where(ok, n_cand, n_at)

        return lax.fori_loop(0, 16, bit_round, (jnp.full((1, tq), I16_MIN, jnp.int32), n_init))

    n_all = jnp.broadcast_to(n_chunks * tk, (1, tq)).astype(jnp.int32)
    v_hi, n_hi_ge = kth_largest16(k_row, n_all)
    v_hi16 = jnp.broadcast_to(v_hi, (tk, tq)).astype(jnp.int16)
    n_hi_gt = count16(lambda v: v > v_hi16)
    k_low = k_row - n_hi_gt

    def low_halves(c, carry):
        key = keys_ref[c]
        low = (key & jnp.int32(0xFFFF)) + I16_MIN
        in_bucket = lax.shift_right_arithmetic(key, 16) == v_hi
        half_ref[c] = jnp.where(in_bucket, low, I16_MIN).astype(jnp.int16)
        return carry
    lax.fori_loop(0, n_chunks, low_halves, 0)

    v_lo, n_lo_ge = kth_largest16(k_low, n_hi_ge - n_hi_gt)
    vstar = lax.shift_left(v_hi, 16) + (v_lo - I16_MIN)

    n_ge = n_hi_gt + n_lo_ge

    @pl.when(jnp.max(n_ge - k_row) > 0)
    def _():
        need = k_row - count(lambda c, k: k > vstar)

        def idx_round(r, x):
            cand = x + lax.shift_left(jnp.int32(1), index_bits - 1 - r)

            def tie_below(c, k):
                s_idx = c * tk + lax.broadcasted_iota(jnp.int32, (tk, tq), 0)
                return (k == vstar) & (s_idx < cand)
            return jnp.where(count(tie_below) < need, cand, x)

        last = lax.fori_loop(0, index_bits, idx_round, jnp.zeros((1, tq), jnp.int32))

        def drop(c, carry):
            k = keys_ref[c]
            s_idx = c * tk + lax.broadcasted_iota(jnp.int32, (tk, tq), 0)
            keys_ref[c] = jnp.where((k == vstar) & (s_idx > last), vstar - 1, k)
            return carry
        lax.fori_loop(0, n_chunks, drop, 0)

    def to_mask(c, carry):
        maskb = jnp.where(keys_ref[c] >= vstar, 0.0, NEG_BIG)
        keys_ref[c] = lax.bitcast_convert_type(maskb, jnp.int32)
        return carry
    lax.fori_loop(0, n_chunks, to_mask, 0)
    keys_ref[dummy_chunk] = lax.bitcast_convert_type(jnp.full((tk, tq), NEG_BIG, jnp.float32), jnp.int32)

    m_ref[...] = jnp.full(m_ref.shape, NEG_BIG, jnp.float32)
    acc_ref[...] = jnp.zeros(acc_ref.shape, jnp.float32)
    p_ref[...] = jnp.zeros(p_ref.shape, p_ref.dtype)
    alpha_ref[...] = jnp.ones(alpha_ref.shape, jnp.float32)
    n_kp = i + 1
    first_near = i - 1

    def pv_stage(cp, buf):
        for h in range(N_HEADS_A):
            rs = slice(V_ROWS * h, V_ROWS * (h + 1))
            pv = jnp.dot(vt_ref[0, cp, rs, :], p_ref[buf, h], preferred_element_type=jnp.float32)
            acc_ref[rs, :] = alpha_ref[buf, h:h + 1, :] * acc_ref[rs, :] + pv

    def qk_stage(cp, buf):
        for j in range(n_pairs):
            d = jnp.dot(ka_ref[0, cp, :, LANES * j:LANES * (j + 1)], qat_ref[0, 0, j],
                        preferred_element_type=jnp.float32)
            lg_ref[buf, 2 * j] = d[:, :tq]
            lg_ref[buf, 2 * j + 1] = d[:, tq:]

    def softmax_stage(cp, valid, buf, near):
        cms = [jnp.where(valid, 2 * cp + e, dummy_chunk) for e in range(2)]
        tbl = jnp.where(valid, jnp.clip(cp - first_near + 1, 0, 2), 0)
        for h in range(N_HEADS_A):
            for hf in range(tq // LANES):
                ls = slice(hf * LANES, (hf + 1) * LANES)
                def masked_logits(r0, nr):
                    e, o = divmod(r0, tk)
                    lm = (lg_ref[buf, h, r0:r0 + nr, ls]
                          + lax.bitcast_convert_type(keys_ref[cms[e], o:o + nr, ls], jnp.float32))
                    if near:
                        lm = lm + bias_ref[tbl, h, r0:r0 + nr, ls]
                    return lm

                m_old = m_ref[h:h + 1, ls]
                m_new = jnp.maximum(m_old, jnp.max(jnp.maximum(masked_logits(0, tk), masked_logits(tk, tk)),
                                                   axis=0, keepdims=True))
                m_ref[h:h + 1, ls] = m_new
                alpha_ref[buf, h:h + 1, ls] = jnp.exp2(m_old - m_new)
                for r0 in range(0, 2 * tk, SOFTMAX_ROWS):
                    p_ref[buf, h, r0:r0 + SOFTMAX_ROWS, ls] = jnp.exp2(
                        masked_logits(r0, SOFTMAX_ROWS) - m_new).astype(p_ref.dtype)

    def step(s, buf, near):
        qk_stage(jnp.minimum(s, n_kp - 1), buf)
        pv_stage(jnp.clip(s - 2, 0, n_kp - 1), buf)
        sm = s - 1
        softmax_stage(jnp.clip(sm, 0, n_kp - 1), (sm >= 0) & (sm < n_kp), 1 - buf, near)

    def steps(near):
        def body(u, carry):
            for k in range(ATT_PAIR_STEPS):
                step(ATT_PAIR_STEPS * u + k, k % 2, near)
            return carry
        return body

    n_bodies = (n_kp + 2 + ATT_PAIR_STEPS - 1) // ATT_PAIR_STEPS
    n_far = jnp.maximum(first_near + 1, 0) // ATT_PAIR_STEPS
    lax.fori_loop(0, n_far, steps(False), 0)
    lax.fori_loop(n_far, n_bodies, steps(True), 0)

    for h in range(N_HEADS_A):
        r0 = V_ROWS * h
        inv = 1.0 / acc_ref[r0 + HEAD_DIM_A:r0 + HEAD_DIM_A + 1, :]
        outt_ref[HEAD_DIM_A * h:HEAD_DIM_A * (h + 1), :] = acc_ref[r0:r0 + HEAD_DIM_A, :] * inv
    out_ref[0] = outt_ref[...].T.astype(out_ref.dtype)


def _bias_tiles(rel_bias):
    tq, tk = ATT_TQ, ATT_TKP
    lo = -(tk - 1)
    dist = np.arange(lo, tk + tq)
    n = np.maximum(dist, 0)
    max_exact = N_BUCKETS // 2
    nf = np.maximum(n, max_exact).astype(np.float64)
    large = max_exact + (np.log(nf / max_exact) / math.log(MAX_DISTANCE / max_exact)
                         * (N_BUCKETS - max_exact)).astype(np.int32)
    bucket = np.where(n < max_exact, n, np.minimum(large, N_BUCKETS - 1)).astype(np.int32)
    vec = ((rel_bias[bucket] - rel_bias[N_BUCKETS - 1][None, :]) * LOG2E).T
    tiles = [jnp.zeros((N_HEADS_A, tk, tq), rel_bias.dtype)]
    for base in (tk, 0):
        rows = [lax.slice_in_dim(vec, base - r - lo, base - r - lo + tq, axis=1) for r in range(tk)]
        tiles.append(jnp.stack(rows, axis=1))
    return jnp.stack(tiles)


def _sparse_attention(kidx4, ka4, vt4, qit2, qat2, wt, bias_tiles):
    b, n_c = kidx4.shape[0], kidx4.shape[1]
    seq = n_c * ATT_TK
    tq = ATT_TQ
    n_pairs = N_HEADS_A // 2
    kern = lambda *refs: _attn_kernel(*refs, index_bits=max(1, (seq - 1).bit_length()))
    return pl.pallas_call(
        kern,
        grid=(b, seq // tq),
        in_specs=[
            _resident((1, n_c, ATT_TK, LANES), lambda bi, i: (bi, 0, 0, 0)),
            _resident((1, n_c // 2, ATT_TKP, D_A), lambda bi, i: (bi, 0, 0, 0)),
            _resident((1, n_c // 2, N_HEADS_A * V_ROWS, ATT_TKP), lambda bi, i: (bi, 0, 0, 0)),
            pl.BlockSpec((1, 1, N_IDX_HEADS // 2, LANES, 2 * tq), lambda bi, i: (bi, i, 0, 0, 0)),
            pl.BlockSpec((1, 1, n_pairs, LANES, 2 * tq), lambda bi, i: (bi, i, 0, 0, 0)),
            pl.BlockSpec((N_IDX_HEADS, tq), lambda bi, i: (0, bi * (seq // tq) + i)),
            _resident((3, N_HEADS_A, ATT_TKP, tq), lambda bi, i: (0, 0, 0, 0)),
        ],
        out_specs=pl.BlockSpec((1, tq, D_A), lambda bi, i: (bi, i, 0)),
        out_shape=jax.ShapeDtypeStruct((b, seq, D_A), jnp.bfloat16),
        scratch_shapes=[
            pltpu.VMEM((n_c + 1, ATT_TK, tq), jnp.int32),
            pltpu.VMEM((n_c + 1, ATT_TK, tq), jnp.int16),
            pltpu.VMEM((2, N_HEADS_A, ATT_TKP, tq), jnp.float32),
            pltpu.VMEM((2, N_HEADS_A, ATT_TKP, tq), jnp.bfloat16),
            pltpu.VMEM((2, N_HEADS_A, tq), jnp.float32),
            pltpu.VMEM((N_HEADS_A, tq), jnp.float32),
            pltpu.VMEM((N_HEADS_A * V_ROWS, tq), jnp.float32),
            pltpu.VMEM((D_A, tq), jnp.float32),
        ],
        compiler_params=pltpu.CompilerParams(
            dimension_semantics=("arbitrary", "arbitrary"), vmem_limit_bytes=VMEM_LIMIT),
        name="indexer_topk_attention",
    )(kidx4, ka4, vt4, qit2, qat2, wt, bias_tiles)


def _ret_kernel(q_ref, k_ref, v_ref, g_ref, cos_ref, sin_ref, dec_ref, qdec_ref, kdec_ref,
                cdec_ref, gn_ref, out_ref, state_ref):
    @pl.when(pl.program_id(1) == 0)
    def _():
        state_ref[...] = jnp.zeros(state_ref.shape, jnp.float32)

    cos = cos_ref[...]
    sin = sin_ref[...]
    half = HEAD_DIM_B // 2
    for h in range(N_HEADS_B):
        lo = HEAD_DIM_B * h
        q = q_ref[0, :, lo:lo + HEAD_DIM_B]
        k = k_ref[0, :, lo:lo + HEAD_DIM_B]
        qr = q * cos + pltpu.roll(q, half, 1) * sin
        ks = (k * cos + pltpu.roll(k, half, 1) * sin) * (HEAD_DIM_B ** -0.5)
        qb = qr.astype(jnp.bfloat16)
        vb = v_ref[0, :, lo:lo + HEAD_DIM_B].astype(jnp.bfloat16)
        qk = lax.dot_general(qb, ks.astype(jnp.bfloat16), (((1,), (1,)), ((), ())),
                             preferred_element_type=jnp.float32)
        inner = (qk * dec_ref[h]).astype(jnp.bfloat16)
        state = state_ref[h]
        y = (jnp.dot(inner, vb, preferred_element_type=jnp.float32)
             + jnp.dot(qb, state.astype(jnp.bfloat16), preferred_element_type=jnp.float32) * qdec_ref[h])
        kd_t = (ks * kdec_ref[h]).T.astype(jnp.bfloat16)
        state_ref[h] = state * cdec_ref[h] + jnp.dot(kd_t, vb, preferred_element_type=jnp.float32)
        yn = y * lax.rsqrt(jnp.mean(y * y, axis=-1, keepdims=True) + EPS) * gn_ref[:, lo:lo + HEAD_DIM_B]
        g = g_ref[0, :, lo:lo + HEAD_DIM_B]
        out_ref[0, :, lo:lo + HEAD_DIM_B] = (yn * (g * jax.nn.sigmoid(g))).astype(out_ref.dtype)


def _retention_tables(seq):
    c = RET_CHUNK
    half = HEAD_DIM_B // 2
    theta = 1.0 / (RET_THETA_BASE ** jnp.linspace(0.0, 1.0, half, dtype=jnp.float32))
    ang = jnp.arange(seq, dtype=jnp.int32).astype(jnp.float32)[:, None] * theta[None, :]
    cos = jnp.concatenate([jnp.cos(ang), jnp.cos(ang)], axis=-1)
    sin = jnp.concatenate([-jnp.sin(ang), jnp.sin(ang)], axis=-1)
    gamma = 1.0 - 2.0 ** (-5.0 - jnp.arange(N_HEADS_B, dtype=jnp.float32))
    log_g = jnp.log(gamma)
    pos = jnp.arange(c, dtype=jnp.float32)
    diff = pos[:, None] - pos[None, :]
    dec = jnp.where(diff >= 0, jnp.exp(log_g[:, None, None] * jnp.maximum(diff, 0.0)), 0.0)
    qdec = jnp.exp(log_g[:, None] * (pos[None, :] + 1.0))
    kdec = jnp.exp(log_g[:, None] * (c - 1.0 - pos[None, :]))
    cdec = jnp.exp(log_g * c)
    bc = lambda a: jnp.broadcast_to(a[:, :, None], (N_HEADS_B, c, HEAD_DIM_B))
    cdec_b = jnp.broadcast_to(cdec[:, None, None], (N_HEADS_B, 1, HEAD_DIM_B))
    return cos, sin, dec, bc(qdec), bc(kdec), cdec_b


def _retention(ob3, gn):
    b, seq, _ = ob3.shape
    c = RET_CHUNK
    cos, sin, dec, qdec, kdec, cdec = _retention_tables(seq)
    col = lambda j: (lambda bi, ci: (bi, ci, j))
    const3 = lambda bi, ci: (0, 0, 0)
    return pl.pallas_call(
        _ret_kernel,
        grid=(b, seq // c),
        in_specs=[
            pl.BlockSpec((1, c, D_B), col(0)),
            pl.BlockSpec((1, c, D_B), col(1)),
            pl.BlockSpec((1, c, D_B), col(2)),
            pl.BlockSpec((1, c, D_B), col(3)),
            pl.BlockSpec((c, HEAD_DIM_B), lambda bi, ci: (ci, 0)),
            pl.BlockSpec((c, HEAD_DIM_B), lambda bi, ci: (ci, 0)),
            _resident((N_HEADS_B, c, c), const3),
            _resident((N_HEADS_B, c, HEAD_DIM_B), const3),
            _resident((N_HEADS_B, c, HEAD_DIM_B), const3),
            _resident((N_HEADS_B, 1, HEAD_DIM_B), const3),
            _resident((1, D_B), lambda bi, ci: (0, 0)),
        ],
        out_specs=pl.BlockSpec((1, c, D_B), lambda bi, ci: (bi, ci, 0)),
        out_shape=jax.ShapeDtypeStruct((b, seq, D_B), jnp.bfloat16),
        scratch_shapes=[pltpu.VMEM((N_HEADS_B, HEAD_DIM_B, HEAD_DIM_B), jnp.float32)],
        compiler_params=pltpu.CompilerParams(
            dimension_semantics=("arbitrary", "arbitrary"), vmem_limit_bytes=VMEM_LIMIT),
        name="retention",
    )(ob3, ob3, ob3, ob3, cos, sin, dec, qdec, kdec, cdec, gn)


def _ffn_kernel(x_ref, a_ref, b_ref, woa_ref, wob_ref, gffn_ref, wg_ref, wu_ref, wd_ref, gfin_ref,
                out_ref):
    x1 = (x_ref[...]
          + jnp.dot(a_ref[...], woa_ref[...], preferred_element_type=jnp.float32)
          + jnp.dot(b_ref[...], wob_ref[...], preferred_element_type=jnp.float32))
    h2 = (x1 * lax.rsqrt(jnp.mean(x1 * x1, axis=-1, keepdims=True) + EPS) * gffn_ref[...]).astype(jnp.bfloat16)
    gate = jnp.dot(h2, wg_ref[...], preferred_element_type=jnp.float32)
    up = jnp.dot(h2, wu_ref[...], preferred_element_type=jnp.float32)
    u = (gate * jax.nn.sigmoid(gate) * up).astype(jnp.bfloat16)
    x2 = x1 + jnp.dot(u, wd_ref[...], preferred_element_type=jnp.float32)
    out_ref[...] = x2 * lax.rsqrt(jnp.mean(x2 * x2, axis=-1, keepdims=True) + EPS) * gfin_ref[...]


def _out_ffn(x2d, a2d, b2d, woa, wob, gffn, wg, wu, wd, gfin):
    rows, d_model = x2d.shape
    d_ff = wg.shape[1]
    tm = FFN_ROWS
    const = lambda i: (0, 0)
    return pl.pallas_call(
        _ffn_kernel,
        grid=(rows // tm,),
        in_specs=[
            pl.BlockSpec((tm, d_model), lambda i: (i, 0)),
            pl.BlockSpec((tm, D_A), lambda i: (i, 0)),
            pl.BlockSpec((tm, D_B), lambda i: (i, 0)),
            _resident((D_A, d_model), const),
            _resident((D_B, d_model), const),
            _resident((1, d_model), const),
            _resident((d_model, d_ff), const),
            _resident((d_model, d_ff), const),
            _resident((d_ff, d_model), const),
            _resident((1, d_model), const),
        ],
        out_specs=pl.BlockSpec((tm, d_model), lambda i: (i, 0)),
        out_shape=jax.ShapeDtypeStruct((rows, d_model), jnp.float32),
        compiler_params=pltpu.CompilerParams(
            dimension_semantics=("arbitrary",), vmem_limit_bytes=VMEM_LIMIT),
        name="out_proj_swiglu",
    )(x2d, a2d, b2d, woa, wob, gffn, wg, wu, wd, gfin)


def kernel(x, norm_mix_g, w_in, idx_k_ln_g, idx_k_ln_b, rel_bias, ret_gn_g, w_out, norm_ffn_g,
           w_gate, w_up, w_down, norm_final_g):
    b, seq, d_model = x.shape
    assert seq % ATT_TQ == 0 and seq % RET_CHUNK == 0 and (b * seq) % PROJ_ROWS == 0
    assert norm_mix_g.shape[0] == 1, "single-layer block"
    bf = jnp.bfloat16
    rows = b * seq
    n_c = seq // ATT_TK
    o = 0
    w = w_in[0]
    qa_w = w[:, o:o + D_A] * (HEAD_DIM_A ** -0.5 * LOG2E); o += D_A
    ka_w = w[:, o:o + D_A]; o += D_A
    va_w = w[:, o:o + D_A]; o += D_A
    qi_w = w[:, o:o + N_IDX_HEADS * IDX_DIM] * (IDX_DIM ** -0.5); o += N_IDX_HEADS * IDX_DIM
    small_w = w[:, o:o + IDX_DIM + N_IDX_HEADS]; o += IDX_DIM + N_IDX_HEADS
    b_w = w[:, o:]
    wa = jnp.concatenate([qa_w, ka_w, va_w, qi_w], axis=1).astype(bf)
    ws = jnp.pad(small_w, ((0, 0), (0, LANES - small_w.shape[1]))).astype(bf)
    wb = b_w.astype(bf)
    lng = jnp.pad(idx_k_ln_g[0], (0, LANES - IDX_DIM)).reshape(1, LANES)
    lnb = jnp.pad(idx_k_ln_b[0], (0, LANES - IDX_DIM)).reshape(1, LANES)

    x2d = x.reshape(rows, d_model)
    qat, qit, ka, vt, kidx, wt, ob = _project(x2d, norm_mix_g[0].reshape(1, d_model), wa, ws, wb, lng, lnb)

    n_q, n_pairs = seq // ATT_TQ, N_HEADS_A // 2
    out_a = _sparse_attention(
        kidx.reshape(b, n_c, ATT_TK, LANES), ka.reshape(b, n_c // 2, ATT_TKP, D_A),
        vt.reshape(b, n_c // 2, N_HEADS_A * V_ROWS, ATT_TKP), qit.reshape(b, n_q, n_pairs, LANES, 2 * ATT_TQ),
        qat.reshape(b, n_q, n_pairs, LANES, 2 * ATT_TQ), wt, _bias_tiles(rel_bias)).reshape(rows, D_A)

    out_b = _retention(ob.reshape(b, seq, 4 * D_B), ret_gn_g[0].reshape(1, D_B)).reshape(rows, D_B)

    wo = w_out[0]
    out = _out_ffn(x2d, out_a, out_b, wo[:D_A].astype(bf), wo[D_A:].astype(bf),
                   norm_ffn_g[0].reshape(1, d_model), w_gate[0].astype(bf), w_up[0].astype(bf),
                   w_down[0].astype(bf), norm_final_g.reshape(1, d_model))
    return out.reshape(b, seq, d_model)
```

```python
import math

import jax
import jax.numpy as jnp
import numpy as np
from jax import lax
from jax.experimental import pallas as pl
from jax.experimental.pallas import tpu as pltpu

N_HEADS_A = 8
HEAD_DIM_A = 64
N_IDX_HEADS = 8
IDX_DIM = 64
TOPK_MAX = 256
N_HEADS_B = 4
HEAD_DIM_B = 128
RET_THETA_BASE = 10000.0
N_BUCKETS = 32
MAX_DISTANCE = 128
EPS = 1e-6

D_A = N_HEADS_A * HEAD_DIM_A
D_B = N_HEADS_B * HEAD_DIM_B
BF16_ROWS = 16
V_ROWS = HEAD_DIM_A + BF16_ROWS
LOG2E = math.log2(math.e)

LANES = 128
MXU_COLS = 256

PROJ_ROWS = 512
ATT_TQ = MXU_COLS
ATT_TK = LANES
COUNT_UNROLL = 8
ATT_STEPS = 4
RET_CHUNK = 256
FFN_ROWS = 512
VMEM_LIMIT = 56 * 1024 * 1024

NEG_BIG = -1e30
INT_MIN = -(2 ** 31)
assert ATT_TQ == 2 * ATT_TK


def _resident(block_shape, index_map):
    return pl.BlockSpec(block_shape, index_map, pipeline_mode=pl.Buffered(1))


def _retention_chunk(ob, cos, sin, dec_ref, qdec_ref, kdec_ref, cdec_ref, gn_ref, state_ref):
    half = HEAD_DIM_B // 2
    outs = []
    for h in range(N_HEADS_B):
        lo = HEAD_DIM_B * h
        q = ob[:, lo:lo + HEAD_DIM_B]
        k = ob[:, D_B + lo:D_B + lo + HEAD_DIM_B]
        qr = q * cos + pltpu.roll(q, half, 1) * sin
        ks = (k * cos + pltpu.roll(k, half, 1) * sin) * (HEAD_DIM_B ** -0.5)
        qb = qr.astype(jnp.bfloat16)
        vb = ob[:, 2 * D_B + lo:2 * D_B + lo + HEAD_DIM_B].astype(jnp.bfloat16)
        qk = lax.dot_general(qb, ks.astype(jnp.bfloat16), (((1,), (1,)), ((), ())),
                             preferred_element_type=jnp.float32)
        inner = (qk * dec_ref[h]).astype(jnp.bfloat16)
        state = state_ref[h]
        y = (jnp.dot(inner, vb, preferred_element_type=jnp.float32)
             + jnp.dot(qb, state.astype(jnp.bfloat16), preferred_element_type=jnp.float32) * qdec_ref[h])
        kd_t = (ks * kdec_ref[h]).T.astype(jnp.bfloat16)
        state_ref[h] = state * cdec_ref[h] + jnp.dot(kd_t, vb, preferred_element_type=jnp.float32)
        yn = y * lax.rsqrt(jnp.mean(y * y, axis=-1, keepdims=True) + EPS) * gn_ref[:, lo:lo + HEAD_DIM_B]
        g = ob[:, 3 * D_B + lo:3 * D_B + lo + HEAD_DIM_B]
        outs.append(yn * (g * jax.nn.sigmoid(g)))
    return jnp.concatenate(outs, axis=-1)


def _proj_kernel(x_ref, g_ref, wa_ref, ws_ref, wb_ref, lng_ref, lnb_ref,
                 cos_ref, sin_ref, dec_ref, qdec_ref, kdec_ref, cdec_ref, gn_ref,
                 qat_ref, qit_ref, ka_ref, vt_ref, okidx_ref, owt_ref, outb_ref, state_ref,
                 *, steps_per_seq):
    bf = jnp.bfloat16
    x = x_ref[...]
    tm = x.shape[0]
    h = x * lax.rsqrt(jnp.mean(x * x, axis=-1, keepdims=True) + EPS) * g_ref[...]
    hb = h.astype(bf)

    def part(k):
        return jnp.dot(hb, wa_ref[:, k * D_A:(k + 1) * D_A], preferred_element_type=jnp.float32)

    ka_ref[...] = part(1).astype(bf)
    va = part(2)
    row = lax.broadcasted_iota(jnp.int32, (V_ROWS - HEAD_DIM_A, ATT_TK), 0)
    ones_rows = jnp.where(row == 0, 1.0, 0.0).astype(bf)
    for r in range(tm // ATT_TK):
        v_t = va[r * ATT_TK:(r + 1) * ATT_TK, :].T.astype(bf)
        for hh in range(N_HEADS_A):
            vt_ref[r, hh * V_ROWS:hh * V_ROWS + HEAD_DIM_A, :] = v_t[hh * HEAD_DIM_A:(hh + 1) * HEAD_DIM_A]
            vt_ref[r, hh * V_ROWS + HEAD_DIM_A:(hh + 1) * V_ROWS, :] = ones_rows
    qa = part(0)
    qi = part(3)
    hd, tq = HEAD_DIM_A, ATT_TQ
    zero_q = jnp.zeros((hd, tq), bf)
    for qb in range(tm // tq):
        qat = qa[qb * tq:(qb + 1) * tq, :].T.astype(bf)
        qit = qi[qb * tq:(qb + 1) * tq, :].T.astype(bf)
        for j in range(N_HEADS_A // 2):
            even = slice(2 * j * hd, (2 * j + 1) * hd)
            odd = slice((2 * j + 1) * hd, (2 * j + 2) * hd)
            qat_ref[qb, j, 0:hd, 0:tq] = qat[even]
            qat_ref[qb, j, hd:2 * hd, 0:tq] = zero_q
            qat_ref[qb, j, 0:hd, tq:2 * tq] = zero_q
            qat_ref[qb, j, hd:2 * hd, tq:2 * tq] = qat[odd]
            qit_ref[qb, j, 0:hd, 0:tq] = qit[even]
            qit_ref[qb, j, 0:hd, tq:2 * tq] = qit[odd]
            qit_ref[qb, j, hd:2 * hd, 0:tq] = zero_q
            qit_ref[qb, j, hd:2 * hd, tq:2 * tq] = zero_q
    @pl.when(pl.program_id(0) % steps_per_seq == 0)
    def _():
        state_ref[...] = jnp.zeros(state_ref.shape, jnp.float32)

    ob = jnp.dot(hb, wb_ref[...], preferred_element_type=jnp.float32)
    for r in range(tm // RET_CHUNK):
        rs = slice(r * RET_CHUNK, (r + 1) * RET_CHUNK)
        outb_ref[rs, :] = _retention_chunk(ob[rs], cos_ref[rs, :], sin_ref[rs, :], dec_ref, qdec_ref,
                                           kdec_ref, cdec_ref, gn_ref, state_ref).astype(outb_ref.dtype)
    s = jnp.dot(hb, ws_ref[...], preferred_element_type=jnp.float32)
    lane = lax.broadcasted_iota(jnp.int32, s.shape, 1)
    is_k = lane < IDX_DIM
    mu = jnp.sum(jnp.where(is_k, s, 0.0), axis=-1, keepdims=True) * (1.0 / IDX_DIM)
    d = jnp.where(is_k, s - mu, 0.0)
    var = jnp.sum(d * d, axis=-1, keepdims=True) * (1.0 / IDX_DIM)
    kn = d * lax.rsqrt(var + EPS) * lng_ref[...] + lnb_ref[...]
    okidx_ref[...] = jnp.where(is_k, kn, 0.0).astype(okidx_ref.dtype)
    owt_ref[...] = (s * (N_IDX_HEADS ** -0.5)).T[IDX_DIM:IDX_DIM + N_IDX_HEADS, :]


def _project(x2, g, wa, ws, wb, lng, lnb, gn, seq):
    rows, d_model = x2.shape
    tm = PROJ_ROWS
    na, nb = wa.shape[1], wb.shape[1]
    n_pairs = N_HEADS_A // 2
    const = lambda i: (0, 0)
    const3 = lambda i: (0, 0, 0)
    steps_per_seq = seq // tm
    cos, sin, dec, qdec, kdec, cdec = _retention_tables(seq)
    kern = lambda *refs: _proj_kernel(*refs, steps_per_seq=steps_per_seq)
    return pl.pallas_call(
        kern,
        grid=(rows // tm,),
        in_specs=[
            pl.BlockSpec((tm, d_model), lambda i: (i, 0)),
            _resident((1, d_model), const),
            _resident((d_model, na), const),
            _resident((d_model, LANES), const),
            _resident((d_model, nb), const),
            _resident((1, LANES), const),
            _resident((1, LANES), const),
            pl.BlockSpec((tm, HEAD_DIM_B), lambda i: (i % steps_per_seq, 0)),
            pl.BlockSpec((tm, HEAD_DIM_B), lambda i: (i % steps_per_seq, 0)),
            _resident((N_HEADS_B, RET_CHUNK, RET_CHUNK), const3),
            _resident((N_HEADS_B, RET_CHUNK, HEAD_DIM_B), const3),
            _resident((N_HEADS_B, RET_CHUNK, HEAD_DIM_B), const3),
            _resident((N_HEADS_B, 1, HEAD_DIM_B), const3),
            _resident((1, D_B), const),
        ],
        out_specs=[
            pl.BlockSpec((tm // ATT_TQ, n_pairs, LANES, 2 * ATT_TQ), lambda i: (i, 0, 0, 0)),
            pl.BlockSpec((tm // ATT_TQ, n_pairs, LANES, 2 * ATT_TQ), lambda i: (i, 0, 0, 0)),
            pl.BlockSpec((tm, D_A), lambda i: (i, 0)),
            pl.BlockSpec((tm // ATT_TK, N_HEADS_A * V_ROWS, ATT_TK), lambda i: (i, 0, 0)),
            pl.BlockSpec((tm, LANES), lambda i: (i, 0)),
            pl.BlockSpec((N_IDX_HEADS, tm), lambda i: (0, i)),
            pl.BlockSpec((tm, D_B), lambda i: (i, 0)),
        ],
        out_shape=[
            jax.ShapeDtypeStruct((rows // ATT_TQ, n_pairs, LANES, 2 * ATT_TQ), jnp.bfloat16),
            jax.ShapeDtypeStruct((rows // ATT_TQ, n_pairs, LANES, 2 * ATT_TQ), jnp.bfloat16),
            jax.ShapeDtypeStruct((rows, D_A), jnp.bfloat16),
            jax.ShapeDtypeStruct((rows // ATT_TK, N_HEADS_A * V_ROWS, ATT_TK), jnp.bfloat16),
            jax.ShapeDtypeStruct((rows, LANES), jnp.bfloat16),
            jax.ShapeDtypeStruct((N_IDX_HEADS, rows), jnp.float32),
            jax.ShapeDtypeStruct((rows, D_B), jnp.bfloat16),
        ],
        scratch_shapes=[pltpu.VMEM((N_HEADS_B, HEAD_DIM_B, HEAD_DIM_B), jnp.float32)],
        compiler_params=pltpu.CompilerParams(
            dimension_semantics=("arbitrary",), vmem_limit_bytes=VMEM_LIMIT),
        name="rmsnorm_in_proj_retention",
    )(x2, g, wa, ws, wb, lng, lnb, cos, sin, dec, qdec, kdec, cdec, gn)


def _attn_kernel(kidx_ref, ka_ref, vt_ref, qit_ref, qat_ref, wt_ref, bias_ref, out_ref,
                 keys_ref, half_ref, lg_ref, p_ref, alpha_ref, m_ref, acc_ref, outt_ref, *, index_bits):
    i = pl.program_id(1)
    tq, tk = ATT_TQ, ATT_TK
    n_pairs = N_HEADS_A // 2
    n_chunks = (i + 1) * (tq // tk)
    dummy_chunk = keys_ref.shape[0] - 1
    t_idx = i * tq + lax.broadcasted_iota(jnp.int32, (1, tq), 1)

    lg_ref[...] = jnp.zeros(lg_ref.shape, jnp.float32)

    def idx_dots(c, buf):
        kc = kidx_ref[0, c]
        for j in range(N_IDX_HEADS // 2):
            d = jnp.dot(kc, qit_ref[0, 0, j], preferred_element_type=jnp.float32)
            lg_ref[buf, 2 * j] = d[:, :tq]
            lg_ref[buf, 2 * j + 1] = d[:, tq:]

    def idx_reduce(c, valid, buf):
        cw = jnp.where(valid, c, dummy_chunk)
        for hf in range(tq // LANES):
            ls = slice(hf * LANES, (hf + 1) * LANES)
            acc = jnp.zeros((tk, LANES), jnp.float32)
            for h in range(N_IDX_HEADS):
                acc = acc + wt_ref[h:h + 1, ls] * jnp.maximum(lg_ref[buf, h, :, ls], 0.0)
            bits = lax.bitcast_convert_type(acc + 0.0, jnp.int32)
            key = jnp.where(bits < 0, bits ^ jnp.int32(0x7FFFFFFF), bits)
            s_idx = c * tk + lax.broadcasted_iota(jnp.int32, (tk, LANES), 0)
            key = jnp.where(s_idx <= t_idx[:, ls], key, jnp.int32(INT_MIN))
            keys_ref[cw, :, ls] = key
            half_ref[cw, :, ls] = lax.shift_right_arithmetic(key, 16).astype(jnp.int16)

    def idx_step(s, buf):
        idx_dots(jnp.minimum(s, n_chunks - 1), buf)
        idx_reduce(jnp.clip(s - 1, 0, n_chunks - 1), (s >= 1) & (s <= n_chunks), 1 - buf)

    def idx_steps(u, carry):
        for k in range(ATT_STEPS):
            idx_step(ATT_STEPS * u + k, k % 2)
        return carry

    lax.fori_loop(0, (n_chunks + 1 + ATT_STEPS - 1) // ATT_STEPS, idx_steps, 0)

    k_row = jnp.minimum(t_idx + 1, TOPK_MAX)
    I16_MIN = -(2 ** 15)

    def count(pred):
        def body(c, acc):
            return acc + jnp.where(pred(c, keys_ref[c]), 1, 0).astype(jnp.int32)
        acc = lax.fori_loop(0, n_chunks, body, jnp.zeros((tk, tq), jnp.int32))
        return jnp.sum(acc, axis=0, keepdims=True)

    one16 = jnp.ones((tk, tq), jnp.int16)
    zero16 = jnp.zeros((tk, tq), jnp.int16)

    def count16(pred):
        def chunks(first, n):
            def body(g, acc):
                for u in range(n):
                    acc = jnp.where(pred(half_ref[first + n * g + u]), acc + one16, acc)
                return acc
            return body
        n_long = n_chunks // COUNT_UNROLL
        acc = lax.fori_loop(0, n_long, chunks(0, COUNT_UNROLL), jnp.zeros((tk, tq), jnp.int16))
        acc = lax.fori_loop(0, (n_chunks - COUNT_UNROLL * n_long) // 2,
                            chunks(COUNT_UNROLL * n_long, 2), acc)
        part = acc[0:BF16_ROWS]
        for r0 in range(BF16_ROWS, tk, BF16_ROWS):
            part = part + acc[r0:r0 + BF16_ROWS]
        return jnp.sum(part.astype(jnp.int32), axis=0, keepdims=True)

    def kth_largest16(k_need, n_init):
        def bit_round(r, st):
            prefix, n_at = st
            cand = prefix + lax.shift_left(jnp.int32(1), 15 - r)
            cand16 = jnp.broadcast_to(cand, (tk, tq)).astype(jnp.int16)
            n_cand = count16(lambda v: v >= cand16)
            ok = n_cand >= k_need
            return jnp.where(ok, cand, prefix), jnp.where(ok, n_cand, n_at)

        return lax.fori_loop(0, 16, bit_round, (jnp.full((1, tq), I16_MIN, jnp.int32), n_init))

    n_all = jnp.broadcast_to(n_chunks * tk, (1, tq)).astype(jnp.int32)
    v_hi, n_hi_ge = kth_largest16(k_row, n_all)
    v_hi16 = jnp.broadcast_to(v_hi, (tk, tq)).astype(jnp.int16)
    n_hi_gt = count16(lambda v: v > v_hi16)
    k_low = k_row - n_hi_gt

    def low_halves(c, carry):
        key = keys_ref[c]
        low = (key & jnp.int32(0xFFFF)) + I16_MIN
        in_bucket = lax.shift_right_arithmetic(key, 16) == v_hi
        half_ref[c] = jnp.where(in_bucket, low, I16_MIN).astype(jnp.int16)
        return carry
    lax.fori_loop(0, n_chunks, low_halves, 0)

    v_lo, n_lo_ge = kth_largest16(k_low, n_hi_ge - n_hi_gt)
    vstar = lax.shift_left(v_hi, 16) + (v_lo - I16_MIN)

    n_ge = n_hi_gt + n_lo_ge

    @pl.when(jnp.max(n_ge - k_row) > 0)
    def _():
        need = k_row - count(lambda c, k: k > vstar)

        def idx_round(r, x):
            cand = x + lax.shift_left(jnp.int32(1), index_bits - 1 - r)

            def tie_below(c, k):
                s_idx = c * tk + lax.broadcasted_iota(jnp.int32, (tk, tq), 0)
                return (k == vstar) & (s_idx < cand)
            return jnp.where(count(tie_below) < need, cand, x)

        last = lax.fori_loop(0, index_bits, idx_round, jnp.zeros((1, tq), jnp.int32))

        def drop(c, carry):
            k = keys_ref[c]
            s_idx = c * tk + lax.broadcasted_iota(jnp.int32, (tk, tq), 0)
            keys_ref[c] = jnp.where((k == vstar) & (s_idx > last), vstar - 1, k)
            return carry
        lax.fori_loop(0, n_chunks, drop, 0)

    def to_mask(c, carry):
        maskb = jnp.where(keys_ref[c] >= vstar, 0.0, NEG_BIG)
        keys_ref[c] = lax.bitcast_convert_type(maskb, jnp.int32)
        return carry
    lax.fori_loop(0, n_chunks, to_mask, 0)
    keys_ref[dummy_chunk] = lax.bitcast_convert_type(jnp.full((tk, tq), NEG_BIG, jnp.float32), jnp.int32)

    m_ref[...] = jnp.full(m_ref.shape, NEG_BIG, jnp.float32)
    acc_ref[...] = jnp.zeros(acc_ref.shape, jnp.float32)
    p_ref[...] = jnp.zeros(p_ref.shape, p_ref.dtype)
    alpha_ref[...] = jnp.ones(alpha_ref.shape, jnp.float32)
    first_near = (tq // tk) * i - 1

    def pv_stage(c, buf):
        for h in range(N_HEADS_A):
            rs = slice(V_ROWS * h, V_ROWS * (h + 1))
            pv = jnp.dot(vt_ref[0, c, rs, :], p_ref[buf, h], preferred_element_type=jnp.float32)
            acc_ref[rs, :] = alpha_ref[buf, h:h + 1, :] * acc_ref[rs, :] + pv

    def qk_stage(c, buf):
        for j in range(n_pairs):
            d = jnp.dot(ka_ref[0, c, :, LANES * j:LANES * (j + 1)], qat_ref[0, 0, j],
                        preferred_element_type=jnp.float32)
            lg_ref[buf, 2 * j] = d[:, :tq]
            lg_ref[buf, 2 * j + 1] = d[:, tq:]

    def softmax_stage(c, valid, buf, near):
        cm = jnp.where(valid, c, dummy_chunk)
        tbl = jnp.where(valid, jnp.clip(c - first_near + 1, 0, 3), 0)
        for h in range(N_HEADS_A):
            for hf in range(tq // LANES):
                ls = slice(hf * LANES, (hf + 1) * LANES)
                lm = lg_ref[buf, h, :, ls] + lax.bitcast_convert_type(keys_ref[cm, :, ls], jnp.float32)
                if near:
                    lm = lm + bias_ref[tbl, h, :, ls]
                m_old = m_ref[h:h + 1, ls]
                m_new = jnp.maximum(m_old, jnp.max(lm, axis=0, keepdims=True))
                alpha = jnp.exp2(m_old - m_new)
                p = jnp.exp2(lm - m_new)
                m_ref[h:h + 1, ls] = m_new
                alpha_ref[buf, h:h + 1, ls] = alpha
                p_ref[buf, h, :, ls] = p.astype(p_ref.dtype)

    def step(s, buf, near):
        qk_stage(jnp.minimum(s, n_chunks - 1), buf)
        pv_stage(jnp.clip(s - 2, 0, n_chunks - 1), buf)
        sm = s - 1
        softmax_stage(jnp.clip(sm, 0, n_chunks - 1), (sm >= 0) & (sm < n_chunks), 1 - buf, near)

    def steps(near):
        def body(u, carry):
            for k in range(ATT_STEPS):
                step(ATT_STEPS * u + k, k % 2, near)
            return carry
        return body

    n_bodies = (n_chunks + 2 + ATT_STEPS - 1) // ATT_STEPS
    n_far = jnp.maximum(first_near + 1, 0) // ATT_STEPS
    lax.fori_loop(0, n_far, steps(False), 0)
    lax.fori_loop(n_far, n_bodies, steps(True), 0)

    for h in range(N_HEADS_A):
        r0 = V_ROWS * h
        inv = 1.0 / acc_ref[r0 + HEAD_DIM_A:r0 + HEAD_DIM_A + 1, :]
        outt_ref[HEAD_DIM_A * h:HEAD_DIM_A * (h + 1), :] = acc_ref[r0:r0 + HEAD_DIM_A, :] * inv
    out_ref[0] = outt_ref[...].T.astype(out_ref.dtype)


def _bias_tiles(rel_bias):
    tq, tk = ATT_TQ, ATT_TK
    lo = -(tq - 1)
    dist = np.arange(lo, tk + tq)
    n = np.maximum(dist, 0)
    max_exact = N_BUCKETS // 2
    nf = np.maximum(n, max_exact).astype(np.float64)
    large = max_exact + (np.log(nf / max_exact) / math.log(MAX_DISTANCE / max_exact)
                         * (N_BUCKETS - max_exact)).astype(np.int32)
    bucket = np.where(n < max_exact, n, np.minimum(large, N_BUCKETS - 1)).astype(np.int32)
    vec = ((rel_bias[bucket] - rel_bias[N_BUCKETS - 1][None, :]) * LOG2E).T
    tiles = [jnp.zeros((N_HEADS_A, tk, tq), rel_bias.dtype)]
    for base in (tk, 0, -tk):
        rows = [lax.slice_in_dim(vec, base - r - lo, base - r - lo + tq, axis=1) for r in range(tk)]
        tiles.append(jnp.stack(rows, axis=1))
    return jnp.stack(tiles)


def _sparse_attention(kidx4, ka4, vt4, qit2, qat2, wt, bias_tiles):
    b, n_c = kidx4.shape[0], kidx4.shape[1]
    seq = n_c * ATT_TK
    tq = ATT_TQ
    n_pairs = N_HEADS_A // 2
    kern = lambda *refs: _attn_kernel(*refs, index_bits=max(1, (seq - 1).bit_length()))
    return pl.pallas_call(
        kern,
        grid=(b, seq // tq),
        in_specs=[
            _resident((1, n_c, ATT_TK, LANES), lambda bi, i: (bi, 0, 0, 0)),
            _resident((1, n_c, ATT_TK, D_A), lambda bi, i: (bi, 0, 0, 0)),
            _resident((1, n_c, N_HEADS_A * V_ROWS, ATT_TK), lambda bi, i: (bi, 0, 0, 0)),
            pl.BlockSpec((1, 1, N_IDX_HEADS // 2, LANES, 2 * tq), lambda bi, i: (bi, i, 0, 0, 0)),
            pl.BlockSpec((1, 1, n_pairs, LANES, 2 * tq), lambda bi, i: (bi, i, 0, 0, 0)),
            pl.BlockSpec((N_IDX_HEADS, tq), lambda bi, i: (0, bi * (seq // tq) + i)),
            _resident((4, N_HEADS_A, ATT_TK, tq), lambda bi, i: (0, 0, 0, 0)),
        ],
        out_specs=pl.BlockSpec((1, tq, D_A), lambda bi, i: (bi, i, 0)),
        out_shape=jax.ShapeDtypeStruct((b, seq, D_A), jnp.bfloat16),
        scratch_shapes=[
            pltpu.VMEM((n_c + 1, ATT_TK, tq), jnp.int32),
            pltpu.VMEM((n_c + 1, ATT_TK, tq), jnp.int16),
            pltpu.VMEM((2, N_HEADS_A, ATT_TK, tq), jnp.float32),
            pltpu.VMEM((2, N_HEADS_A, ATT_TK, tq), jnp.bfloat16),
            pltpu.VMEM((2, N_HEADS_A, tq), jnp.float32),
            pltpu.VMEM((N_HEADS_A, tq), jnp.float32),
            pltpu.VMEM((N_HEADS_A * V_ROWS, tq), jnp.float32),
            pltpu.VMEM((D_A, tq), jnp.float32),
        ],
        compiler_params=pltpu.CompilerParams(
            dimension_semantics=("arbitrary", "arbitrary"), vmem_limit_bytes=VMEM_LIMIT),
        name="indexer_topk_attention",
    )(kidx4, ka4, vt4, qit2, qat2, wt, bias_tiles)


def _retention_tables(seq):
    c = RET_CHUNK
    half = HEAD_DIM_B // 2
    theta = 1.0 / (RET_THETA_BASE ** jnp.linspace(0.0, 1.0, half, dtype=jnp.float32))
    ang = jnp.arange(seq, dtype=jnp.int32).astype(jnp.float32)[:, None] * theta[None, :]
    cos = jnp.concatenate([jnp.cos(ang), jnp.cos(ang)], axis=-1)
    sin = jnp.concatenate([-jnp.sin(ang), jnp.sin(ang)], axis=-1)
    gamma = 1.0 - 2.0 ** (-5.0 - jnp.arange(N_HEADS_B, dtype=jnp.float32))
    log_g = jnp.log(gamma)
    pos = jnp.arange(c, dtype=jnp.float32)
    diff = pos[:, None] - pos[None, :]
    dec = jnp.where(diff >= 0, jnp.exp(log_g[:, None, None] * jnp.maximum(diff, 0.0)), 0.0)
    qdec = jnp.exp(log_g[:, None] * (pos[None, :] + 1.0))
    kdec = jnp.exp(log_g[:, None] * (c - 1.0 - pos[None, :]))
    cdec = jnp.exp(log_g * c)
    bc = lambda a: jnp.broadcast_to(a[:, :, None], (N_HEADS_B, c, HEAD_DIM_B))
    cdec_b = jnp.broadcast_to(cdec[:, None, None], (N_HEADS_B, 1, HEAD_DIM_B))
    return cos, sin, dec, bc(qdec), bc(kdec), cdec_b


def _ffn_kernel(x_ref, a_ref, b_ref, woa_ref, wob_ref, gffn_ref, wg_ref, wu_ref, wd_ref, gfin_ref,
                out_ref):
    x1 = (x_ref[...]
          + jnp.dot(a_ref[...], woa_ref[...], preferred_element_type=jnp.float32)
          + jnp.dot(b_ref[...], wob_ref[...], preferred_element_type=jnp.float32))
    h2 = (x1 * lax.rsqrt(jnp.mean(x1 * x1, axis=-1, keepdims=True) + EPS) * gffn_ref[...]).astype(jnp.bfloat16)
    gate = jnp.dot(h2, wg_ref[...], preferred_element_type=jnp.float32)
    up = jnp.dot(h2, wu_ref[...], preferred_element_type=jnp.float32)
    u = (gate * jax.nn.sigmoid(gate) * up).astype(jnp.bfloat16)
    x2 = x1 + jnp.dot(u, wd_ref[...], preferred_element_type=jnp.float32)
    out_ref[...] = x2 * lax.rsqrt(jnp.mean(x2 * x2, axis=-1, keepdims=True) + EPS) * gfin_ref[...]


def _out_ffn(x2d, a2d, b2d, woa, wob, gffn, wg, wu, wd, gfin):
    rows, d_model = x2d.shape
    d_ff = wg.shape[1]
    tm = FFN_ROWS
    const = lambda i: (0, 0)
    return pl.pallas_call(
        _ffn_kernel,
        grid=(rows // tm,),
        in_specs=[
            pl.BlockSpec((tm, d_model), lambda i: (i, 0)),
            pl.BlockSpec((tm, D_A), lambda i: (i, 0)),
            pl.BlockSpec((tm, D_B), lambda i: (i, 0)),
            _resident((D_A, d_model), const),
            _resident((D_B, d_model), const),
            _resident((1, d_model), const),
            _resident((d_model, d_ff), const),
            _resident((d_model, d_ff), const),
            _resident((d_ff, d_model), const),
            _resident((1, d_model), const),
        ],
        out_specs=pl.BlockSpec((tm, d_model), lambda i: (i, 0)),
        out_shape=jax.ShapeDtypeStruct((rows, d_model), jnp.float32),
        compiler_params=pltpu.CompilerParams(
            dimension_semantics=("arbitrary",), vmem_limit_bytes=VMEM_LIMIT),
        name="out_proj_swiglu",
    )(x2d, a2d, b2d, woa, wob, gffn, wg, wu, wd, gfin)


def kernel(x, norm_mix_g, w_in, idx_k_ln_g, idx_k_ln_b, rel_bias, ret_gn_g, w_out, norm_ffn_g,
           w_gate, w_up, w_down, norm_final_g):
    b, seq, d_model = x.shape
    assert seq % ATT_TQ == 0 and seq % PROJ_ROWS == 0 and PROJ_ROWS % RET_CHUNK == 0
    assert (b * seq) % FFN_ROWS == 0
    assert norm_mix_g.shape[0] == 1, "single-layer block"
    bf = jnp.bfloat16
    rows = b * seq
    n_c = seq // ATT_TK
    o = 0
    w = w_in[0]
    qa_w = w[:, o:o + D_A] * (HEAD_DIM_A ** -0.5 * LOG2E); o += D_A
    ka_w = w[:, o:o + D_A]; o += D_A
    va_w = w[:, o:o + D_A]; o += D_A
    qi_w = w[:, o:o + N_IDX_HEADS * IDX_DIM] * (IDX_DIM ** -0.5); o += N_IDX_HEADS * IDX_DIM
    small_w = w[:, o:o + IDX_DIM + N_IDX_HEADS]; o += IDX_DIM + N_IDX_HEADS
    b_w = w[:, o:]
    wa = jnp.concatenate([qa_w, ka_w, va_w, qi_w], axis=1).astype(bf)
    ws = jnp.pad(small_w, ((0, 0), (0, LANES - small_w.shape[1]))).astype(bf)
    wb = b_w.astype(bf)
    lng = jnp.pad(idx_k_ln_g[0], (0, LANES - IDX_DIM)).reshape(1, LANES)
    lnb = jnp.pad(idx_k_ln_b[0], (0, LANES - IDX_DIM)).reshape(1, LANES)

    x2d = x.reshape(rows, d_model)
    qat, qit, ka, vt, kidx, wt, out_b = _project(x2d, norm_mix_g[0].reshape(1, d_model), wa, ws, wb, lng, lnb,
                                                 ret_gn_g[0].reshape(1, D_B), seq)

    n_q, n_pairs = seq // ATT_TQ, N_HEADS_A // 2
    out_a = _sparse_attention(
        kidx.reshape(b, n_c, ATT_TK, LANES), ka.reshape(b, n_c, ATT_TK, D_A),
        vt.reshape(b, n_c, N_HEADS_A * V_ROWS, ATT_TK), qit.reshape(b, n_q, n_pairs, LANES, 2 * ATT_TQ),
        qat.reshape(b, n_q, n_pairs, LANES, 2 * ATT_TQ), wt, _bias_tiles(rel_bias)).reshape(rows, D_A)

    wo = w_out[0]
    out = _out_ffn(x2d, out_a, out_b, wo[:D_A].astype(bf), wo[D_A:].astype(bf),
                   norm_ffn_g[0].reshape(1, d_model), w_gate[0].astype(bf), w_up[0].astype(bf),
                   w_down[0].astype(bf), norm_final_g.reshape(1, d_model))
    return out.reshape(b, seq, d_model)
```

```python
import math

import jax
import jax.numpy as jnp
import numpy as np
from jax import lax
from jax.experimental import pallas as pl
from jax.experimental.pallas import tpu as pltpu

N_HEADS_A = 8
HEAD_DIM_A = 64
N_IDX_HEADS = 8
IDX_DIM = 64
TOPK_MAX = 256
N_HEADS_B = 4
HEAD_DIM_B = 128
RET_THETA_BASE = 10000.0
N_BUCKETS = 32
MAX_DISTANCE = 128
EPS = 1e-6

D_A = N_HEADS_A * HEAD_DIM_A
D_B = N_HEADS_B * HEAD_DIM_B
BF16_ROWS = 16
V_ROWS = HEAD_DIM_A + BF16_ROWS
LOG2E = math.log2(math.e)

LANES = 128
MXU_COLS = 256

PROJ_ROWS = 512
ATT_TQ = MXU_COLS
ATT_TK = LANES
COUNT_UNROLL = 8
IDX_STEPS = 8
ATT_STEPS = 8
RET_CHUNK = 256
FFN_ROWS = 512
VMEM_LIMIT = 56 * 1024 * 1024

NEG_BIG = -1e30
INT_MIN = -(2 ** 31)
assert ATT_TQ == 2 * ATT_TK


def _resident(block_shape, index_map):
    return pl.BlockSpec(block_shape, index_map, pipeline_mode=pl.Buffered(1))


def _retention_chunk(ob, cos, sin, dec_ref, qdec_ref, kdec_ref, cdec_ref, gn_ref, state_ref):
    half = HEAD_DIM_B // 2
    outs = []
    for h in range(N_HEADS_B):
        lo = HEAD_DIM_B * h
        q = ob[:, lo:lo + HEAD_DIM_B]
        k = ob[:, D_B + lo:D_B + lo + HEAD_DIM_B]
        qr = q * cos + pltpu.roll(q, half, 1) * sin
        ks = (k * cos + pltpu.roll(k, half, 1) * sin) * (HEAD_DIM_B ** -0.5)
        qb = qr.astype(jnp.bfloat16)
        vb = ob[:, 2 * D_B + lo:2 * D_B + lo + HEAD_DIM_B].astype(jnp.bfloat16)
        qk = lax.dot_general(qb, ks.astype(jnp.bfloat16), (((1,), (1,)), ((), ())),
                             preferred_element_type=jnp.float32)
        inner = (qk * dec_ref[h]).astype(jnp.bfloat16)
        state = state_ref[h]
        y = (jnp.dot(inner, vb, preferred_element_type=jnp.float32)
             + jnp.dot(qb, state.astype(jnp.bfloat16), preferred_element_type=jnp.float32) * qdec_ref[h])
        kd_t = (ks * kdec_ref[h]).T.astype(jnp.bfloat16)
        state_ref[h] = state * cdec_ref[h] + jnp.dot(kd_t, vb, preferred_element_type=jnp.float32)
        yn = y * lax.rsqrt(jnp.mean(y * y, axis=-1, keepdims=True) + EPS) * gn_ref[:, lo:lo + HEAD_DIM_B]
        g = ob[:, 3 * D_B + lo:3 * D_B + lo + HEAD_DIM_B]
        outs.append(yn * (g * jax.nn.sigmoid(g)))
    return jnp.concatenate(outs, axis=-1)


def _proj_kernel(x_ref, g_ref, wa_ref, ws_ref, wb_ref, lng_ref, lnb_ref,
                 cos_ref, sin_ref, dec_ref, qdec_ref, kdec_ref, cdec_ref, gn_ref,
                 qat_ref, qit_ref, ka_ref, vt_ref, okidx_ref, owt_ref, outb_ref, state_ref,
                 *, steps_per_seq):
    bf = jnp.bfloat16
    x = x_ref[...]
    tm = x.shape[0]
    h = x * lax.rsqrt(jnp.mean(x * x, axis=-1, keepdims=True) + EPS) * g_ref[...]
    hb = h.astype(bf)

    def part(k):
        return jnp.dot(hb, wa_ref[:, k * D_A:(k + 1) * D_A], preferred_element_type=jnp.float32)

    ka_ref[...] = part(1).astype(bf)
    va = part(2)
    row = lax.broadcasted_iota(jnp.int32, (V_ROWS - HEAD_DIM_A, ATT_TK), 0)
    ones_rows = jnp.where(row == 0, 1.0, 0.0).astype(bf)
    for r in range(tm // ATT_TK):
        v_t = va[r * ATT_TK:(r + 1) * ATT_TK, :].T.astype(bf)
        for hh in range(N_HEADS_A):
            vt_ref[r, hh * V_ROWS:hh * V_ROWS + HEAD_DIM_A, :] = v_t[hh * HEAD_DIM_A:(hh + 1) * HEAD_DIM_A]
            vt_ref[r, hh * V_ROWS + HEAD_DIM_A:(hh + 1) * V_ROWS, :] = ones_rows
    qa = part(0)
    qi = part(3)
    hd, tq = HEAD_DIM_A, ATT_TQ
    zero_q = jnp.zeros((hd, tq), bf)
    for qb in range(tm // tq):
        qat = qa[qb * tq:(qb + 1) * tq, :].T.astype(bf)
        qit = qi[qb * tq:(qb + 1) * tq, :].T.astype(bf)
        for j in range(N_HEADS_A // 2):
            even = slice(2 * j * hd, (2 * j + 1) * hd)
            odd = slice((2 * j + 1) * hd, (2 * j + 2) * hd)
            qat_ref[qb, j, 0:hd, 0:tq] = qat[even]
            qat_ref[qb, j, hd:2 * hd, 0:tq] = zero_q
            qat_ref[qb, j, 0:hd, tq:2 * tq] = zero_q
            qat_ref[qb, j, hd:2 * hd, tq:2 * tq] = qat[odd]
            qit_ref[qb, j, 0:hd, 0:tq] = qit[even]
            qit_ref[qb, j, 0:hd, tq:2 * tq] = qit[odd]
            qit_ref[qb, j, hd:2 * hd, 0:tq] = zero_q
            qit_ref[qb, j, hd:2 * hd, tq:2 * tq] = zero_q
    @pl.when(pl.program_id(0) % steps_per_seq == 0)
    def _():
        state_ref[...] = jnp.zeros(state_ref.shape, jnp.float32)

    ob = jnp.dot(hb, wb_ref[...], preferred_element_type=jnp.float32)
    for r in range(tm // RET_CHUNK):
        rs = slice(r * RET_CHUNK, (r + 1) * RET_CHUNK)
        outb_ref[rs, :] = _retention_chunk(ob[rs], cos_ref[rs, :], sin_ref[rs, :], dec_ref, qdec_ref,
                                           kdec_ref, cdec_ref, gn_ref, state_ref).astype(outb_ref.dtype)
    s = jnp.dot(hb, ws_ref[...], preferred_element_type=jnp.float32)
    lane = lax.broadcasted_iota(jnp.int32, s.shape, 1)
    is_k = lane < IDX_DIM
    mu = jnp.sum(jnp.where(is_k, s, 0.0), axis=-1, keepdims=True) * (1.0 / IDX_DIM)
    d = jnp.where(is_k, s - mu, 0.0)
    var = jnp.sum(d * d, axis=-1, keepdims=True) * (1.0 / IDX_DIM)
    kn = d * lax.rsqrt(var + EPS) * lng_ref[...] + lnb_ref[...]
    okidx_ref[...] = jnp.where(is_k, kn, 0.0).astype(okidx_ref.dtype)
    owt_ref[...] = (s * (N_IDX_HEADS ** -0.5)).T[IDX_DIM:IDX_DIM + N_IDX_HEADS, :]


def _project(x2, g, wa, ws, wb, lng, lnb, gn, seq):
    rows, d_model = x2.shape
    tm = PROJ_ROWS
    na, nb = wa.shape[1], wb.shape[1]
    n_pairs = N_HEADS_A // 2
    const = lambda i: (0, 0)
    const3 = lambda i: (0, 0, 0)
    steps_per_seq = seq // tm
    cos, sin, dec, qdec, kdec, cdec = _retention_tables(seq)
    kern = lambda *refs: _proj_kernel(*refs, steps_per_seq=steps_per_seq)
    return pl.pallas_call(
        kern,
        grid=(rows // tm,),
        in_specs=[
            pl.BlockSpec((tm, d_model), lambda i: (i, 0)),
            _resident((1, d_model), const),
            _resident((d_model, na), const),
            _resident((d_model, LANES), const),
            _resident((d_model, nb), const),
            _resident((1, LANES), const),
            _resident((1, LANES), const),
            pl.BlockSpec((tm, HEAD_DIM_B), lambda i: (i % steps_per_seq, 0)),
            pl.BlockSpec((tm, HEAD_DIM_B), lambda i: (i % steps_per_seq, 0)),
            _resident((N_HEADS_B, RET_CHUNK, RET_CHUNK), const3),
            _resident((N_HEADS_B, RET_CHUNK, HEAD_DIM_B), const3),
            _resident((N_HEADS_B, RET_CHUNK, HEAD_DIM_B), const3),
            _resident((N_HEADS_B, 1, HEAD_DIM_B), const3),
            _resident((1, D_B), const),
        ],
        out_specs=[
            pl.BlockSpec((tm // ATT_TQ, n_pairs, LANES, 2 * ATT_TQ), lambda i: (i, 0, 0, 0)),
            pl.BlockSpec((tm // ATT_TQ, n_pairs, LANES, 2 * ATT_TQ), lambda i: (i, 0, 0, 0)),
            pl.BlockSpec((tm, D_A), lambda i: (i, 0)),
            pl.BlockSpec((tm // ATT_TK, N_HEADS_A * V_ROWS, ATT_TK), lambda i: (i, 0, 0)),
            pl.BlockSpec((tm, LANES), lambda i: (i, 0)),
            pl.BlockSpec((N_IDX_HEADS, tm), lambda i: (0, i)),
            pl.BlockSpec((tm, D_B), lambda i: (i, 0)),
        ],
        out_shape=[
            jax.ShapeDtypeStruct((rows // ATT_TQ, n_pairs, LANES, 2 * ATT_TQ), jnp.bfloat16),
            jax.ShapeDtypeStruct((rows // ATT_TQ, n_pairs, LANES, 2 * ATT_TQ), jnp.bfloat16),
            jax.ShapeDtypeStruct((rows, D_A), jnp.bfloat16),
            jax.ShapeDtypeStruct((rows // ATT_TK, N_HEADS_A * V_ROWS, ATT_TK), jnp.bfloat16),
            jax.ShapeDtypeStruct((rows, LANES), jnp.bfloat16),
            jax.ShapeDtypeStruct((N_IDX_HEADS, rows), jnp.float32),
            jax.ShapeDtypeStruct((rows, D_B), jnp.bfloat16),
        ],
        scratch_shapes=[pltpu.VMEM((N_HEADS_B, HEAD_DIM_B, HEAD_DIM_B), jnp.float32)],
        compiler_params=pltpu.CompilerParams(
            dimension_semantics=("arbitrary",), vmem_limit_bytes=VMEM_LIMIT),
        name="rmsnorm_in_proj_retention",
    )(x2, g, wa, ws, wb, lng, lnb, cos, sin, dec, qdec, kdec, cdec, gn)


def _attn_kernel(kidx_ref, ka_ref, vt_ref, qit_ref, qat_ref, wt_ref, bias_ref, out_ref,
                 keys_ref, half_ref, lg_ref, p_ref, alpha_ref, m_ref, acc_ref, outt_ref, *, index_bits):
    i = pl.program_id(1)
    tq, tk = ATT_TQ, ATT_TK
    n_pairs = N_HEADS_A // 2
    n_chunks = (i + 1) * (tq // tk)
    dummy_chunk = keys_ref.shape[0] - 1
    t_idx = i * tq + lax.broadcasted_iota(jnp.int32, (1, tq), 1)

    lg_ref[...] = jnp.zeros(lg_ref.shape, jnp.float32)

    def idx_dots(c, buf):
        kc = kidx_ref[0, c]
        for j in range(N_IDX_HEADS // 2):
            d = jnp.dot(kc, qit_ref[0, 0, j], preferred_element_type=jnp.float32)
            lg_ref[buf, 2 * j] = d[:, :tq]
            lg_ref[buf, 2 * j + 1] = d[:, tq:]

    def idx_reduce(c, valid, buf):
        cw = jnp.where(valid, c, dummy_chunk)
        for hf in range(tq // LANES):
            ls = slice(hf * LANES, (hf + 1) * LANES)
            acc = jnp.zeros((tk, LANES), jnp.float32)
            for h in range(N_IDX_HEADS):
                acc = acc + wt_ref[h:h + 1, ls] * jnp.maximum(lg_ref[buf, h, :, ls], 0.0)
            bits = lax.bitcast_convert_type(acc + 0.0, jnp.int32)
            key = jnp.where(bits < 0, bits ^ jnp.int32(0x7FFFFFFF), bits)
            s_idx = c * tk + lax.broadcasted_iota(jnp.int32, (tk, LANES), 0)
            key = jnp.where(s_idx <= t_idx[:, ls], key, jnp.int32(INT_MIN))
            keys_ref[cw, :, ls] = key
            half_ref[cw, :, ls] = lax.shift_right_arithmetic(key, 16).astype(jnp.int16)

    def idx_step(s, buf):
        idx_dots(jnp.minimum(s, n_chunks - 1), buf)
        idx_reduce(jnp.clip(s - 1, 0, n_chunks - 1), (s >= 1) & (s <= n_chunks), 1 - buf)

    def idx_steps(u, carry):
        for k in range(IDX_STEPS):
            idx_step(IDX_STEPS * u + k, k % 2)
        return carry

    lax.fori_loop(0, (n_chunks + 1 + IDX_STEPS - 1) // IDX_STEPS, idx_steps, 0)

    k_row = jnp.minimum(t_idx + 1, TOPK_MAX)
    I16_MIN = -(2 ** 15)

    def count(pred):
        def body(c, acc):
            return acc + jnp.where(pred(c, keys_ref[c]), 1, 0).astype(jnp.int32)
        acc = lax.fori_loop(0, n_chunks, body, jnp.zeros((tk, tq), jnp.int32))
        return jnp.sum(acc, axis=0, keepdims=True)

    one16 = jnp.ones((tk, tq), jnp.int16)
    zero16 = jnp.zeros((tk, tq), jnp.int16)

    def count16(pred):
        def chunks(first, n):
            def body(g, acc):
                for u in range(n):
                    acc = jnp.where(pred(half_ref[first + n * g + u]), acc + one16, acc)
                return acc
            return body
        n_long = n_chunks // COUNT_UNROLL
        acc = lax.fori_loop(0, n_long, chunks(0, COUNT_UNROLL), jnp.zeros((tk, tq), jnp.int16))
        acc = lax.fori_loop(0, (n_chunks - COUNT_UNROLL * n_long) // 2,
                            chunks(COUNT_UNROLL * n_long, 2), acc)
        part = acc[0:BF16_ROWS]
        for r0 in range(BF16_ROWS, tk, BF16_ROWS):
            part = part + acc[r0:r0 + BF16_ROWS]
        return jnp.sum(part.astype(jnp.int32), axis=0, keepdims=True)

    def kth_largest16(k_need, n_init):
        def bit_round(r, st):
            prefix, n_at = st
            cand = prefix + lax.shift_left(jnp.int32(1), 15 - r)
            cand16 = jnp.broadcast_to(cand, (tk, tq)).astype(jnp.int16)
            n_cand = count16(lambda v: v >= cand16)
            ok = n_cand >= k_need
            return jnp.where(ok, cand, prefix), jnp.where(ok, n_cand, n_at)

        return lax.fori_loop(0, 16, bit_round, (jnp.full((1, tq), I16_MIN, jnp.int32), n_init))

    n_all = jnp.broadcast_to(n_chunks * tk, (1, tq)).astype(jnp.int32)
    v_hi, n_hi_ge = kth_largest16(k_row, n_all)
    v_hi16 = jnp.broadcast_to(v_hi, (tk, tq)).astype(jnp.int16)
    n_hi_gt = count16(lambda v: v > v_hi16)
    k_low = k_row - n_hi_gt

    def low_halves(c, carry):
        key = keys_ref[c]
        low = (key & jnp.int32(0xFFFF)) + I16_MIN
        in_bucket = lax.shift_right_arithmetic(key, 16) == v_hi
        half_ref[c] = jnp.where(in_bucket, low, I16_MIN).astype(jnp.int16)
        return carry
    lax.fori_loop(0, n_chunks, low_halves, 0)

    v_lo, n_lo_ge = kth_largest16(k_low, n_hi_ge - n_hi_gt)
    vstar = lax.shift_left(v_hi, 16) + (v_lo - I16_MIN)

    n_ge = n_hi_gt + n_lo_ge

    @pl.when(jnp.max(n_ge - k_row) > 0)
    def _():
        need = k_row - count(lambda c, k: k > vstar)

        def idx_round(r, x):
            cand = x + lax.shift_left(jnp.int32(1), index_bits - 1 - r)

            def tie_below(c, k):
                s_idx = c * tk + lax.broadcasted_iota(jnp.int32, (tk, tq), 0)
                return (k == vstar) & (s_idx < cand)
            return jnp.where(count(tie_below) < need, cand, x)

        last = lax.fori_loop(0, index_bits, idx_round, jnp.zeros((1, tq), jnp.int32))

        def drop(c, carry):
            k = keys_ref[c]
            s_idx = c * tk + lax.broadcasted_iota(jnp.int32, (tk, tq), 0)
            keys_ref[c] = jnp.where((k == vstar) & (s_idx > last), vstar - 1, k)
            return carry
        lax.fori_loop(0, n_chunks, drop, 0)

    def to_mask(c, carry):
        maskb = jnp.where(keys_ref[c] >= vstar, 0.0, NEG_BIG)
        keys_ref[c] = lax.bitcast_convert_type(maskb, jnp.int32)
        return carry
    lax.fori_loop(0, n_chunks, to_mask, 0)
    keys_ref[dummy_chunk] = lax.bitcast_convert_type(jnp.full((tk, tq), NEG_BIG, jnp.float32), jnp.int32)

    m_ref[...] = jnp.full(m_ref.shape, NEG_BIG, jnp.float32)
    acc_ref[...] = jnp.zeros(acc_ref.shape, jnp.float32)
    p_ref[...] = jnp.zeros(p_ref.shape, p_ref.dtype)
    alpha_ref[...] = jnp.ones(alpha_ref.shape, jnp.float32)
    first_near = (tq // tk) * i - 1

    def pv_stage(c, buf):
        for h in range(N_HEADS_A):
            rs = slice(V_ROWS * h, V_ROWS * (h + 1))
            pv = jnp.dot(vt_ref[0, c, rs, :], p_ref[buf, h], preferred_element_type=jnp.float32)
            acc_ref[rs, :] = alpha_ref[buf, h:h + 1, :] * acc_ref[rs, :] + pv

    def qk_stage(c, buf):
        for j in range(n_pairs):
            d = jnp.dot(ka_ref[0, c, :, LANES * j:LANES * (j + 1)], qat_ref[0, 0, j],
                        preferred_element_type=jnp.float32)
            lg_ref[buf, 2 * j] = d[:, :tq]
            lg_ref[buf, 2 * j + 1] = d[:, tq:]

    def softmax_stage(c, valid, buf, near):
        cm = jnp.where(valid, c, dummy_chunk)
        tbl = jnp.where(valid, jnp.clip(c - first_near + 1, 0, 3), 0)
        for h in range(N_HEADS_A):
            for hf in range(tq // LANES):
                ls = slice(hf * LANES, (hf + 1) * LANES)
                lm = lg_ref[buf, h, :, ls] + lax.bitcast_convert_type(keys_ref[cm, :, ls], jnp.float32)
                if near:
                    lm = lm + bias_ref[tbl, h, :, ls]
                m_old = m_ref[h:h + 1, ls]
                m_new = jnp.maximum(m_old, jnp.max(lm, axis=0, keepdims=True))
                alpha = jnp.exp2(m_old - m_new)
                p = jnp.exp2(lm - m_new)
                m_ref[h:h + 1, ls] = m_new
                alpha_ref[buf, h:h + 1, ls] = alpha
                p_ref[buf, h, :, ls] = p.astype(p_ref.dtype)

    def step(s, buf, near):
        qk_stage(jnp.minimum(s, n_chunks - 1), buf)
        pv_stage(jnp.clip(s - 2, 0, n_chunks - 1), buf)
        sm = s - 1
        softmax_stage(jnp.clip(sm, 0, n_chunks - 1), (sm >= 0) & (sm < n_chunks), 1 - buf, near)

    def steps(near):
        def body(u, carry):
            for k in range(ATT_STEPS):
                step(ATT_STEPS * u + k, k % 2, near)
            return carry
        return body

    n_bodies = (n_chunks + 2 + ATT_STEPS - 1) // ATT_STEPS
    n_far = jnp.maximum(first_near + 1, 0) // ATT_STEPS
    lax.fori_loop(0, n_far, steps(False), 0)
    lax.fori_loop(n_far, n_bodies, steps(True), 0)

    for h in range(N_HEADS_A):
        r0 = V_ROWS * h
        inv = 1.0 / acc_ref[r0 + HEAD_DIM_A:r0 + HEAD_DIM_A + 1, :]
        outt_ref[HEAD_DIM_A * h:HEAD_DIM_A * (h + 1), :] = acc_ref[r0:r0 + HEAD_DIM_A, :] * inv
    out_ref[0] = outt_ref[...].T.astype(out_ref.dtype)


def _bias_tiles(rel_bias):
    tq, tk = ATT_TQ, ATT_TK
    lo = -(tq - 1)
    dist = np.arange(lo, tk + tq)
    n = np.maximum(dist, 0)
    max_exact = N_BUCKETS // 2
    nf = np.maximum(n, max_exact).astype(np.float64)
    large = max_exact + (np.log(nf / max_exact) / math.log(MAX_DISTANCE / max_exact)
                         * (N_BUCKETS - max_exact)).astype(np.int32)
    bucket = np.where(n < max_exact, n, np.minimum(large, N_BUCKETS - 1)).astype(np.int32)
    vec = ((rel_bias[bucket] - rel_bias[N_BUCKETS - 1][None, :]) * LOG2E).T
    tiles = [jnp.zeros((N_HEADS_A, tk, tq), rel_bias.dtype)]
    for base in (tk, 0, -tk):
        rows = [lax.slice_in_dim(vec, base - r - lo, base - r - lo + tq, axis=1) for r in range(tk)]
        tiles.append(jnp.stack(rows, axis=1))
    return jnp.stack(tiles)


def _sparse_attention(kidx4, ka4, vt4, qit2, qat2, wt, bias_tiles):
    b, n_c = kidx4.shape[0], kidx4.shape[1]
    seq = n_c * ATT_TK
    tq = ATT_TQ
    n_pairs = N_HEADS_A // 2
    kern = lambda *refs: _attn_kernel(*refs, index_bits=max(1, (seq - 1).bit_length()))
    return pl.pallas_call(
        kern,
        grid=(b, seq // tq),
        in_specs=[
            _resident((1, n_c, ATT_TK, LANES), lambda bi, i: (bi, 0, 0, 0)),
            _resident((1, n_c, ATT_TK, D_A), lambda bi, i: (bi, 0, 0, 0)),
            _resident((1, n_c, N_HEADS_A * V_ROWS, ATT_TK), lambda bi, i: (bi, 0, 0, 0)),
            pl.BlockSpec((1, 1, N_IDX_HEADS // 2, LANES, 2 * tq), lambda bi, i: (bi, i, 0, 0, 0)),
            pl.BlockSpec((1, 1, n_pairs, LANES, 2 * tq), lambda bi, i: (bi, i, 0, 0, 0)),
            pl.BlockSpec((N_IDX_HEADS, tq), lambda bi, i: (0, bi * (seq // tq) + i)),
            _resident((4, N_HEADS_A, ATT_TK, tq), lambda bi, i: (0, 0, 0, 0)),
        ],
        out_specs=pl.BlockSpec((1, tq, D_A), lambda bi, i: (bi, i, 0)),
        out_shape=jax.ShapeDtypeStruct((b, seq, D_A), jnp.bfloat16),
        scratch_shapes=[
            pltpu.VMEM((n_c + 1, ATT_TK, tq), jnp.int32),
            pltpu.VMEM((n_c + 1, ATT_TK, tq), jnp.int16),
            pltpu.VMEM((2, N_HEADS_A, ATT_TK, tq), jnp.float32),
            pltpu.VMEM((2, N_HEADS_A, ATT_TK, tq), jnp.bfloat16),
            pltpu.VMEM((2, N_HEADS_A, tq), jnp.float32),
            pltpu.VMEM((N_HEADS_A, tq), jnp.float32),
            pltpu.VMEM((N_HEADS_A * V_ROWS, tq), jnp.float32),
            pltpu.VMEM((D_A, tq), jnp.float32),
        ],
        compiler_params=pltpu.CompilerParams(
            dimension_semantics=("arbitrary", "arbitrary"), vmem_limit_bytes=VMEM_LIMIT),
        name="indexer_topk_attention",
    )(kidx4, ka4, vt4, qit2, qat2, wt, bias_tiles)


def _retention_tables(seq):
    c = RET_CHUNK
    half = HEAD_DIM_B // 2
    theta = 1.0 / (RET_THETA_BASE ** jnp.linspace(0.0, 1.0, half, dtype=jnp.float32))
    ang = jnp.arange(seq, dtype=jnp.int32).astype(jnp.float32)[:, None] * theta[None, :]
    cos = jnp.concatenate([jnp.cos(ang), jnp.cos(ang)], axis=-1)
    sin = jnp.concatenate([-jnp.sin(ang), jnp.sin(ang)], axis=-1)
    gamma = 1.0 - 2.0 ** (-5.0 - jnp.arange(N_HEADS_B, dtype=jnp.float32))
    log_g = jnp.log(gamma)
    pos = jnp.arange(c, dtype=jnp.float32)
    diff = pos[:, None] - pos[None, :]
    dec = jnp.where(diff >= 0, jnp.exp(log_g[:, None, None] * jnp.maximum(diff, 0.0)), 0.0)
    qdec = jnp.exp(log_g[:, None] * (pos[None, :] + 1.0))
    kdec = jnp.exp(log_g[:, None] * (c - 1.0 - pos[None, :]))
    cdec = jnp.exp(log_g * c)
    bc = lambda a: jnp.broadcast_to(a[:, :, None], (N_HEADS_B, c, HEAD_DIM_B))
    cdec_b = jnp.broadcast_to(cdec[:, None, None], (N_HEADS_B, 1, HEAD_DIM_B))
    return cos, sin, dec, bc(qdec), bc(kdec), cdec_b


def _ffn_kernel(x_ref, a_ref, b_ref, woa_ref, wob_ref, gffn_ref, wg_ref, wu_ref, wd_ref, gfin_ref,
                out_ref):
    x1 = (x_ref[...]
          + jnp.dot(a_ref[...], woa_ref[...], preferred_element_type=jnp.float32)
          + jnp.dot(b_ref[...], wob_ref[...], preferred_element_type=jnp.float32))
    h2 = (x1 * lax.rsqrt(jnp.mean(x1 * x1, axis=-1, keepdims=True) + EPS) * gffn_ref[...]).astype(jnp.bfloat16)
    gate = jnp.dot(h2, wg_ref[...], preferred_element_type=jnp.float32)
    up = jnp.dot(h2, wu_ref[...], preferred_element_type=jnp.float32)
    u = (gate * jax.nn.sigmoid(gate) * up).astype(jnp.bfloat16)
    x2 = x1 + jnp.dot(u, wd_ref[...], preferred_element_type=jnp.float32)
    out_ref[...] = x2 * lax.rsqrt(jnp.mean(x2 * x2, axis=-1, keepdims=True) + EPS) * gfin_ref[...]


def _out_ffn(x2d, a2d, b2d, woa, wob, gffn, wg, wu, wd, gfin):
    rows, d_model = x2d.shape
    d_ff = wg.shape[1]
    tm = FFN_ROWS
    const = lambda i: (0, 0)
    return pl.pallas_call(
        _ffn_kernel,
        grid=(rows // tm,),
        in_specs=[
            pl.BlockSpec((tm, d_model), lambda i: (i, 0)),
            pl.BlockSpec((tm, D_A), lambda i: (i, 0)),
            pl.BlockSpec((tm, D_B), lambda i: (i, 0)),
            _resident((D_A, d_model), const),
            _resident((D_B, d_model), const),
            _resident((1, d_model), const),
            _resident((d_model, d_ff), const),
            _resident((d_model, d_ff), const),
            _resident((d_ff, d_model), const),
            _resident((1, d_model), const),
        ],
        out_specs=pl.BlockSpec((tm, d_model), lambda i: (i, 0)),
        out_shape=jax.ShapeDtypeStruct((rows, d_model), jnp.float32),
        compiler_params=pltpu.CompilerParams(
            dimension_semantics=("arbitrary",), vmem_limit_bytes=VMEM_LIMIT),
        name="out_proj_swiglu",
    )(x2d, a2d, b2d, woa, wob, gffn, wg, wu, wd, gfin)


def kernel(x, norm_mix_g, w_in, idx_k_ln_g, idx_k_ln_b, rel_bias, ret_gn_g, w_out, norm_ffn_g,
           w_gate, w_up, w_down, norm_final_g):
    b, seq, d_model = x.shape
    assert seq % ATT_TQ == 0 and seq % PROJ_ROWS == 0 and PROJ_ROWS % RET_CHUNK == 0
    assert (b * seq) % FFN_ROWS == 0
    assert norm_mix_g.shape[0] == 1, "single-layer block"
    bf = jnp.bfloat16
    rows = b * seq
    n_c = seq // ATT_TK
    o = 0
    w = w_in[0]
    qa_w = w[:, o:o + D_A] * (HEAD_DIM_A ** -0.5 * LOG2E); o += D_A
    ka_w = w[:, o:o + D_A]; o += D_A
    va_w = w[:, o:o + D_A]; o += D_A
    qi_w = w[:, o:o + N_IDX_HEADS * IDX_DIM] * (IDX_DIM ** -0.5); o += N_IDX_HEADS * IDX_DIM
    small_w = w[:, o:o + IDX_DIM + N_IDX_HEADS]; o += IDX_DIM + N_IDX_HEADS
    b_w = w[:, o:]
    wa = jnp.concatenate([qa_w, ka_w, va_w, qi_w], axis=1).astype(bf)
    ws = jnp.pad(small_w, ((0, 0), (0, LANES - small_w.shape[1]))).astype(bf)
    wb = b_w.astype(bf)
    lng = jnp.pad(idx_k_ln_g[0], (0, LANES - IDX_DIM)).reshape(1, LANES)
    lnb = jnp.pad(idx_k_ln_b[0], (0, LANES - IDX_DIM)).reshape(1, LANES)

    x2d = x.reshape(rows, d_model)
    qat, qit, ka, vt, kidx, wt, out_b = _project(x2d, norm_mix_g[0].reshape(1, d_model), wa, ws, wb, lng, lnb,
                                                 ret_gn_g[0].reshape(1, D_B), seq)

    n_q, n_pairs = seq // ATT_TQ, N_HEADS_A // 2
    out_a = _sparse_attention(
        kidx.reshape(b, n_c, ATT_TK, LANES), ka.reshape(b, n_c, ATT_TK, D_A),
        vt.reshape(b, n_c, N_HEADS_A * V_ROWS, ATT_TK), qit.reshape(b, n_q, n_pairs, LANES, 2 * ATT_TQ),
        qat.reshape(b, n_q, n_pairs, LANES, 2 * ATT_TQ), wt, _bias_tiles(rel_bias)).reshape(rows, D_A)

    wo = w_out[0]
    out = _out_ffn(x2d, out_a, out_b, wo[:D_A].astype(bf), wo[D_A:].astype(bf),
                   norm_ffn_g[0].reshape(1, d_model), w_gate[0].astype(bf), w_up[0].astype(bf),
                   w_down[0].astype(bf), norm_final_g.reshape(1, d_model))
    return out.reshape(b, seq, d_model)
```
